```python
import math
import jax, jax.numpy as jnp
from jax import lax
import numpy as np

D_MODEL = 1024
BATCH = 16
SEQ = 2048
DEPTH = 2
DEC_BATCH = 32
DEC_SEQ = 8
PAST_LEN = 16384
PAGE_SIZE = 128

H_RET = 8
RET_DK = 64
RET_DV = 128
H_FOX = 8
FOX_HD = 64
H_RWKV = 8
RWKV_N = 64
RWKV_C = H_RWKV * RWKV_N
LORA_W = 64
LORA_A = 64
LORA_G = 128
D_FF = 2816
N_BRANCH = 3
Q_BLOCK = 128
RET_CHUNK = 128
NORM_EPS = 1e-6
RWKV_GN_EPS = 64e-5
ROPE_BASE = 10000.0

RET_W_QK = H_RET * RET_DK
RET_W_V = H_RET * RET_DV
FOX_W = H_FOX * FOX_HD
RWKV_IN = 3 * RWKV_C + LORA_W + LORA_A + LORA_G
GROUP_SIZES = (RET_W_QK, RET_W_QK, RET_W_V, RET_W_V, FOX_W, FOX_W, FOX_W, H_FOX, RWKV_IN, N_BRANCH * D_MODEL)
RWKV_OFF = 2 * RET_W_QK + 2 * RET_W_V + 3 * FOX_W + H_FOX
D_IN = RWKV_OFF + RWKV_IN + N_BRANCH * D_MODEL
RWKV_SIZES = (RWKV_C, RWKV_C, RWKV_C, LORA_W, LORA_A, LORA_G)

kernel_name = "hybrid_ret_fox_rwkv7_macaron_step"

F32 = jnp.float32


def split_cols(p, sizes):
    idx = np.cumsum(np.array(sizes))[:-1].tolist()
    return jnp.split(p, idx, axis=-1)


def rmsnorm(x, g):
    x32 = x.astype(F32)
    y = x32 * lax.rsqrt(jnp.mean(x32 * x32, axis=-1, keepdims=True) + NORM_EPS) * g.astype(F32)
    return y.astype(x.dtype)


def modnorm(x, g, shift, scale):
    return rmsnorm(x, g) * (1 + scale) + shift


def swiglu(h, w_in, w_out):
    a, b = jnp.split(h @ w_in, 2, axis=-1)
    return (jax.nn.silu(a) * b) @ w_out


def rotary(x, pos):
    half = x.shape[-1] // 2
    inv = ROPE_BASE ** (-jnp.arange(half, dtype=F32) / half)
    ang = pos[:, None] * inv[None, :]
    cos = jnp.cos(ang)[None, :, None, :]
    sin = jnp.sin(ang)[None, :, None, :]
    x1 = x[..., :half].astype(F32)
    x2 = x[..., half:].astype(F32)
    return jnp.concatenate([x1 * cos - x2 * sin, x1 * sin + x2 * cos], axis=-1)


def retention(q, k, v, state0):
    B, T, H, DK = q.shape
    DV = v.shape[-1]
    L = min(RET_CHUNK, T)
    n = T // L
    log_g = jnp.log1p(-jnp.exp2(-5.0 - jnp.arange(H, dtype=F32)))
    idx = jnp.arange(L, dtype=F32)
    rel = idx[:, None] - idx[None, :]
    decay_mat = jnp.where(rel >= 0, jnp.exp(log_g[:, None, None] * jnp.maximum(rel, 0.0)), 0.0)
    q_dec = jnp.exp(log_g[:, None] * (idx[None, :] + 1.0)).T[None, :, :, None]
    k_dec = jnp.exp(log_g[:, None] * (L - 1.0 - idx[None, :])).T[None, :, :, None]
    chunk_dec = jnp.exp(log_g * L)[None, :, None, None]

    def chunks(t, d):
        return t.reshape(B, n, L, H, d).transpose(1, 0, 2, 3, 4)

    def step(S, inp):
        qn, kn, vn = inp
        inter = jnp.einsum('blhk,bhkv->blhv', qn, S) * q_dec
        scores = jnp.einsum('bihk,bjhk->bhij', qn, kn) * decay_mat[None]
        intra = jnp.einsum('bhij,bjhv->bihv', scores, vn)
        S = S * chunk_dec + jnp.einsum('bjhk,bjhv->bhkv', kn * k_dec, vn)
        return S, inter + intra

    S, out = lax.scan(step, state0, (chunks(q, DK), chunks(k, DK), chunks(v, DV)))
    return out.transpose(1, 0, 2, 3, 4).reshape(B, T, H, DV), S


def fox_prompt(q, k, v, logf):
    B, T, H, D = q.shape
    cum = jnp.cumsum(logf, axis=1).transpose(0, 2, 1)
    scale = D ** -0.5
    kpos = jnp.arange(T)

    def block(i):
        s0 = i * Q_BLOCK
        qb = lax.dynamic_slice_in_dim(q, s0, Q_BLOCK, axis=1)
        cb = lax.dynamic_slice_in_dim(cum, s0, Q_BLOCK, axis=2)
        logits = jnp.einsum('bqhd,bkhd->bhqk', qb, k, preferred_element_type=F32) * scale
        logits = logits + cb[..., None] - cum[:, :, None, :]
        qpos = s0 + jnp.arange(Q_BLOCK)
        logits = jnp.where(kpos[None, :] <= qpos[:, None], logits, -jnp.inf)
        p = jax.nn.softmax(logits, axis=-1)
        return jnp.einsum('bhqk,bkhd->bqhd', p.astype(v.dtype), v)

    out = lax.map(block, jnp.arange(T // Q_BLOCK))
    return out.transpose(1, 0, 2, 3, 4).reshape(B, T, H, D)


def fox_sample(q, k, v, logf, k_past, v_past, logf_past):
    T, D = q.shape[1], q.shape[-1]
    scale = D ** -0.5
    lp = logf_past.astype(F32)
    suffix = (lax.cumsum(lp, axis=1, reverse=True) - lp).transpose(0, 2, 1)
    cum_new = jnp.cumsum(logf, axis=1).transpose(0, 2, 1)
    lg_past = jnp.einsum('bqhd,bkhd->bhqk', q, k_past, preferred_element_type=F32) * scale
    lg_past = lg_past + suffix[:, :, None, :] + cum_new[..., None]
    lg_new = jnp.einsum('bqhd,bkhd->bhqk', q, k, preferred_element_type=F32) * scale
    lg_new = lg_new + cum_new[..., None] - cum_new[:, :, None, :]
    tri = jnp.arange(T)[None, :] <= jnp.arange(T)[:, None]
    lg_new = jnp.where(tri, lg_new, -jnp.inf)
    p = jax.nn.softmax(jnp.concatenate([lg_past, lg_new], axis=-1), axis=-1)
    P = k_past.shape[1]
    return (jnp.einsum('bhqk,bkhd->bqhd', p[..., :P].astype(v.dtype), v_past)
            + jnp.einsum('bhqk,bkhd->bqhd', p[..., P:].astype(v.dtype), v))


def wkv7_scan(r, w, k, v, a, b, S0):
    def step(S, inp):
        rt, wt, kt, vt, at, bt = inp
        sa = jnp.einsum('bhij,bhj->bhi', S, at)
        S = S * wt[:, :, None, :] + sa[..., None] * bt[:, :, None, :] + vt[..., None] * kt[:, :, None, :]
        return S, jnp.einsum('bhij,bhj->bhi', S, rt)

    xs = tuple(t.transpose(1, 0, 2, 3) for t in (r, w, k, v, a, b))
    S, ys = lax.scan(step, S0, xs)
    return ys.transpose(1, 0, 2, 3), S


def rwkv7(cols, prev_cols, S0, l, W):
    B, T, _ = cols.shape
    P = cols.astype(F32)
    Pp = jnp.concatenate([prev_cols.astype(F32), P[:, :-1]], axis=1)
    Pm = P + (Pp - P) * W['rwkv_mu'][l]
    xr, xk, xv, xw, xa, xg = split_cols(Pm, RWKV_SIZES)
    w_raw = -jax.nn.softplus(-(W['rwkv_w0'][l] + jnp.tanh(xw) @ W['rwkv_w2'][l])) - 0.5
    decay = jnp.exp(-jnp.exp(w_raw))
    a = jax.nn.sigmoid(W['rwkv_a0'][l] + xa @ W['rwkv_a2'][l])
    g = jax.nn.sigmoid(xg) @ W['rwkv_g2'][l]

    def hs(t):
        return t.reshape(B, T, H_RWKV, RWKV_N)

    kk = hs(xk * W['rwkv_k_k'][l])
    kk = kk / jnp.maximum(jnp.sqrt(jnp.sum(kk * kk, axis=-1, keepdims=True)), 1e-12)
    kmod = xk * (1 + (a - 1) * W['rwkv_k_a'][l])
    r, kh, vh, dh, ah = hs(xr), hs(kmod), hs(xv), hs(decay), hs(a)
    y, S = wkv7_scan(r, dh, kh, vh, -kk, kk * ah, S0.astype(F32))
    mean = jnp.mean(y, axis=-1, keepdims=True)
    var = jnp.mean((y - mean) ** 2, axis=-1, keepdims=True)
    y = ((y - mean) * lax.rsqrt(var + RWKV_GN_EPS)).reshape(B, T, RWKV_C) * W['rwkv_ln_w'][l] + W['rwkv_ln_b'][l]
    bonus = jnp.sum(r * kh * W['rwkv_r_k'][l], axis=-1, keepdims=True) * vh
    y = y + bonus.reshape(B, T, RWKV_C)
    return (y * g).astype(cols.dtype), S


def token_mix(h, pos, l, W, ret_S0, rwkv_prev_h, rwkv_S0, fox_past):
    B, T, _ = h.shape
    w_in = W['w_in'][l]
    rq, rk, rv, rg, fq, fk, fv, ff, rw, gcol = split_cols(h @ w_in, GROUP_SIZES)
    q = rotary(rq.reshape(B, T, H_RET, RET_DK), pos)
    k = rotary(rk.reshape(B, T, H_RET, RET_DK), pos) * (RET_DK ** -0.5)
    v = rv.reshape(B, T, H_RET, RET_DV).astype(F32)
    o, ret_S = retention(q, k, v, ret_S0.astype(F32))
    o = o * lax.rsqrt(jnp.mean(o * o, axis=-1, keepdims=True) + NORM_EPS)
    o_ret = (jax.nn.silu(rg.astype(F32)) * o.reshape(B, T, RET_W_V)).astype(h.dtype)
    fox_q = fq.reshape(B, T, H_FOX, FOX_HD)
    fox_k = fk.reshape(B, T, H_FOX, FOX_HD)
    fox_v = fv.reshape(B, T, H_FOX, FOX_HD)
    logf = jax.nn.log_sigmoid((ff + W['b_fox_f'][l]).astype(F32))
    if fox_past is None:
        of = fox_prompt(fox_q, fox_k, fox_v, logf)
    else:
        of = fox_sample(fox_q, fox_k, fox_v, logf, fox_past[0], fox_past[1], fox_past[2])
    o_fox = of.reshape(B, T, FOX_W)
    if rwkv_prev_h is None:
        prev = jnp.zeros_like(rw[:, :1])
    else:
        prev = (rwkv_prev_h @ w_in[:, RWKV_OFF:RWKV_OFF + RWKV_IN])[:, None, :]
    o_rwkv, rwkv_S = rwkv7(rw, prev, rwkv_S0, l, W)
    gates = jax.nn.sigmoid(gcol).reshape(B, T, N_BRANCH, D_MODEL)
    merged = (gates[:, :, 0] * (o_ret @ W['w_br_ret'][l])
              + gates[:, :, 1] * (o_fox @ W['w_br_fox'][l])
              + gates[:, :, 2] * (o_rwkv @ W['w_br_rwkv'][l]))
    out = merged @ W['w_o'][l]
    return out, (fox_k, fox_v, logf, ret_S, rwkv_S, h[:, -1])


def decoder_layer(x, c, pos, l, W, ret_S0, rwkv_prev_h, rwkv_S0, fox_past):
    B = x.shape[0]
    m = (jax.nn.silu(c) @ W['w_ada'][l] + W['b_ada'][l]).reshape(B, 1, 9, D_MODEL)
    sh1, sc1, g1, sh2, sc2, g2, sh3, sc3, g3 = [m[:, :, i] for i in range(9)]
    h = modnorm(x, W['norm_g'][l, 0], sh1, sc1)
    x = x + 0.5 * g1 * swiglu(h, W['w_ffn_in'][l, 0], W['w_ffn_out'][l, 0])
    h = modnorm(x, W['norm_g'][l, 1], sh2, sc2)
    mix, st = token_mix(h, pos, l, W, ret_S0, rwkv_prev_h, rwkv_S0, fox_past)
    x = x + g2 * mix
    h = modnorm(x, W['norm_g'][l, 2], sh3, sc3)
    x = x + 0.5 * g3 * swiglu(h, W['w_ffn_in'][l, 1], W['w_ffn_out'][l, 1])
    return x, st


def setup_inputs(seed: int = 0) -> dict:
    key = jax.random.key(seed)
    ks = iter(jax.random.split(key, 64))

    def nrm(shape, s=1.0):
        return s * jax.random.normal(next(ks), shape, F32)

    def uni(shape, lo, hi):
        return jax.random.uniform(next(ks), shape, F32, lo, hi)

    n_pages = PAST_LEN // PAGE_SIZE
    n_pool = (DEC_BATCH * n_pages * 5) // 4
    perm = jax.random.permutation(next(ks), n_pool)
    page_table = perm[:DEC_BATCH * n_pages].reshape(DEC_BATCH, n_pages).astype(jnp.int32)
    D = D_MODEL
    fox_head_bias = jnp.linspace(3.0, 10.0, H_FOX, dtype=F32)
    return {
        'x_prompt': nrm((BATCH, SEQ, D)),
        'x_sample': nrm((DEC_BATCH, DEC_SEQ, D)),
        'cache_fox_k': nrm((DEPTH, n_pool, PAGE_SIZE, H_FOX, FOX_HD)),
        'cache_fox_v': nrm((DEPTH, n_pool, PAGE_SIZE, H_FOX, FOX_HD)),
        'cache_fox_logf': jax.nn.log_sigmoid(fox_head_bias + nrm((DEPTH, n_pool, PAGE_SIZE, H_FOX), 0.5)),
        'state_ret': nrm((DEPTH, DEC_BATCH, H_RET, RET_DK, RET_DV)),
        'state_rwkv_wkv': nrm((DEPTH, DEC_BATCH, H_RWKV, RWKV_N, RWKV_N), 0.5),
        'state_rwkv_shift': nrm((DEPTH, DEC_BATCH, D)),
        'page_table': page_table,
        'c_prompt': nrm((BATCH, D)),
        'c_sample': nrm((DEC_BATCH, D)),
        'w_ada': nrm((DEPTH, D, 9 * D), 0.5 * D ** -0.5),
        'b_ada': nrm((DEPTH, 9 * D), 0.02),
        'norm_g': 1.0 + nrm((DEPTH, 3, D), 0.05),
        'w_ffn_in': nrm((DEPTH, 2, D, 2 * D_FF), D ** -0.5),
        'w_ffn_out': nrm((DEPTH, 2, D_FF, D), D_FF ** -0.5),
        'w_in': nrm((DEPTH, D, D_IN), D ** -0.5),
        'b_fox_f': fox_head_bias + nrm((DEPTH, H_FOX), 0.3),
        'rwkv_mu': uni((DEPTH, RWKV_IN), 0.0, 1.0),
        'rwkv_w0': uni((DEPTH, RWKV_C), -6.0, -1.0),
        'rwkv_w2': nrm((DEPTH, LORA_W, RWKV_C), 0.5 * LORA_W ** -0.5),
        'rwkv_a0': nrm((DEPTH, RWKV_C), 0.1),
        'rwkv_a2': nrm((DEPTH, LORA_A, RWKV_C), 0.5 * LORA_A ** -0.5),
        'rwkv_g2': nrm((DEPTH, LORA_G, RWKV_C), LORA_G ** -0.5),
        'rwkv_k_k': 0.85 + nrm((DEPTH, RWKV_C), 0.05),
        'rwkv_k_a': 1.0 + nrm((DEPTH, RWKV_C), 0.05),
        'rwkv_r_k': nrm((DEPTH, H_RWKV, RWKV_N), 0.1),
        'rwkv_ln_w': 1.0 + nrm((DEPTH, RWKV_C), 0.05),
        'rwkv_ln_b': nrm((DEPTH, RWKV_C), 0.02),
        'w_br_ret': nrm((DEPTH, RET_W_V, D), RET_W_V ** -0.5),
        'w_br_fox': nrm((DEPTH, FOX_W, D), FOX_W ** -0.5),
        'w_br_rwkv': nrm((DEPTH, RWKV_C, D), RWKV_C ** -0.5),
        'w_o': nrm((DEPTH, D, D), D ** -0.5),
        'final_norm_g': 1.0 + nrm((D,), 0.05),
    }


def reference(x_prompt, x_sample, cache_fox_k, cache_fox_v, cache_fox_logf, state_ret, state_rwkv_wkv,
              state_rwkv_shift, page_table, c_prompt, c_sample, w_ada, b_ada, norm_g, w_ffn_in, w_ffn_out,
              w_in, b_fox_f, rwkv_mu, rwkv_w0, rwkv_w2, rwkv_a0, rwkv_a2, rwkv_g2, rwkv_k_k, rwkv_k_a,
              rwkv_r_k, rwkv_ln_w, rwkv_ln_b, w_br_ret, w_br_fox, w_br_rwkv, w_o, final_norm_g):
    W = dict(w_ada=w_ada, b_ada=b_ada, norm_g=norm_g, w_ffn_in=w_ffn_in, w_ffn_out=w_ffn_out, w_in=w_in,
             b_fox_f=b_fox_f, rwkv_mu=rwkv_mu, rwkv_w0=rwkv_w0, rwkv_w2=rwkv_w2, rwkv_a0=rwkv_a0,
             rwkv_a2=rwkv_a2, rwkv_g2=rwkv_g2, rwkv_k_k=rwkv_k_k, rwkv_k_a=rwkv_k_a, rwkv_r_k=rwkv_r_k,
             rwkv_ln_w=rwkv_ln_w, rwkv_ln_b=rwkv_ln_b, w_br_ret=w_br_ret, w_br_fox=w_br_fox,
             w_br_rwkv=w_br_rwkv, w_o=w_o)
    B, T = x_prompt.shape[:2]
    DB, TS = x_sample.shape[:2]
    past_len = page_table.shape[1] * cache_fox_k.shape[2]
    pos_p = jnp.arange(T, dtype=F32)
    pos_s = jnp.arange(TS, dtype=F32) + past_len
    xp, xs = x_prompt, x_sample
    st_p, st_s = [], []
    for l in range(DEPTH):
        xp, sp = decoder_layer(xp, c_prompt, pos_p, l, W,
                               jnp.zeros((B, H_RET, RET_DK, RET_DV), F32), None,
                               jnp.zeros((B, H_RWKV, RWKV_N, RWKV_N), F32), None)
        fox_past = (cache_fox_k[l][page_table].reshape(DB, past_len, H_FOX, FOX_HD),
                    cache_fox_v[l][page_table].reshape(DB, past_len, H_FOX, FOX_HD),
                    cache_fox_logf[l][page_table].reshape(DB, past_len, H_FOX))
        xs, ss = decoder_layer(xs, c_sample, pos_s, l, W, state_ret[l], state_rwkv_shift[l],
                               state_rwkv_wkv[l], fox_past)
        st_p.append(sp)
        st_s.append(ss)
    y_prompt = rmsnorm(xp, final_norm_g)
    y_sample = rmsnorm(xs, final_norm_g)
    fox_k_p = jnp.stack([s[0] for s in st_p], 0)
    fox_v_p = jnp.stack([s[1] for s in st_p], 0)
    fox_logf_p = jnp.stack([s[2] for s in st_p], 0)
    ret_state_p = jnp.stack([s[3] for s in st_p], 0)
    rwkv_wkv_p = jnp.stack([s[4] for s in st_p], 0)
    rwkv_shift_p = jnp.stack([s[5] for s in st_p], 0)
    fox_k_s = jnp.stack([s[0] for s in st_s], 0)
    fox_v_s = jnp.stack([s[1] for s in st_s], 0)
    fox_logf_s = jnp.stack([s[2] for s in st_s], 0)
    ret_state_s = jnp.stack([s[3] for s in st_s], 0)
    rwkv_wkv_s = jnp.stack([s[4] for s in st_s], 0)
    rwkv_shift_s = jnp.stack([s[5] for s in st_s], 0)
    return (y_prompt, y_sample, fox_k_p, fox_v_p, fox_logf_p, ret_state_p, rwkv_wkv_p, rwkv_shift_p,
            fox_k_s, fox_v_s, fox_logf_s, ret_state_s, rwkv_wkv_s, rwkv_shift_s)
```

```python
import functools
import math

import jax
import jax.numpy as jnp
from jax import lax
from jax.experimental import pallas as pl
from jax.experimental.pallas import tpu as pltpu

F32 = jnp.float32
BF16 = jnp.bfloat16
HIGHEST = lax.Precision.HIGHEST

NORM_EPS = 1e-6
RWKV_GN_EPS = 64e-5
ROPE_BASE = 10000.0
NEG_BIG = -1e30

LANES = 128
MIB = 1 << 20
NT_DIMS = (((1,), (1,)), ((), ()))
TN_DIMS = (((0,), (0,)), ((), ()))


def _cparams(n_grid, vmem_mib):
    return pltpu.CompilerParams(
        dimension_semantics=("arbitrary",) * n_grid,
        vmem_limit_bytes=int(vmem_mib * MIB),
    )


def _dot(a, b):
    return jnp.dot(a, b, preferred_element_type=F32)


def _sigmoid(x):
    return 1.0 / (1.0 + jnp.exp(-x))


def _silu(x):
    return x * _sigmoid(x)


def _softplus(x):
    return jnp.maximum(x, 0.0) + jnp.log1p(jnp.exp(-jnp.abs(x)))


def _modnorm(x, g, shift, scale):
    y = x * lax.rsqrt(jnp.mean(x * x, axis=-1, keepdims=True) + NORM_EPS) * g
    return y * (1.0 + scale) + shift


def _ada_kernel(c_ref, w_ref, b_ref, o_ref):
    s = _silu(c_ref[...]).astype(BF16)
    o_ref[...] = _dot(s, w_ref[...].astype(BF16)) + b_ref[...]


def _ada(c_all, w_ada, b_ada):
    depth, d, n = w_ada.shape
    rows = c_all.shape[0]
    tn = d
    return pl.pallas_call(
        _ada_kernel,
        out_shape=jax.ShapeDtypeStruct((depth, rows, n), F32),
        grid=(depth, n // tn),
        in_specs=[
            pl.BlockSpec((rows, d), lambda l, j: (0, 0)),
            pl.BlockSpec((None, d, tn), lambda l, j: (l, 0, j)),
            pl.BlockSpec((None, 1, tn), lambda l, j: (l, 0, j)),
        ],
        out_specs=pl.BlockSpec((None, rows, tn), lambda l, j: (l, 0, j)),
        compiler_params=_cparams(2, 24),
        name="ada",
    )(c_all, w_ada, b_ada.reshape(depth, 1, n))


def _mod_spec(mod, tm, rows_per_seq, d, n_grid):
    if mod.ndim == 3:
        tiles_per_seq = rows_per_seq // tm
        if n_grid == 1:
            return pl.BlockSpec((None, 1, d), lambda i: (i // tiles_per_seq, 0, 0))
        return pl.BlockSpec((None, 1, d), lambda i, j: (i // tiles_per_seq, 0, 0))
    if n_grid == 1:
        return pl.BlockSpec((tm, d), lambda i: (i, 0))
    return pl.BlockSpec((tm, d), lambda i, j: (i, 0))


def _ffn_kernel(x_ref, sh_ref, sc_ref, gt_ref, ng_ref, w1_ref, w2_ref, o_ref, h_ref, acc_ref, *, dff, chunk):
    x = x_ref[...]
    h_ref[...] = _modnorm(x, ng_ref[...], sh_ref[...], sc_ref[...]).astype(BF16)
    acc_ref[...] = jnp.zeros_like(acc_ref)

    def body(c, carry):
        off = pl.multiple_of(c * chunk, LANES)
        off_b = pl.multiple_of(dff + c * chunk, LANES)
        h = h_ref[...]
        a = _dot(h, w1_ref[:, pl.ds(off, chunk)])
        b = _dot(h, w1_ref[:, pl.ds(off_b, chunk)])
        z = (_silu(a) * b).astype(BF16)
        acc_ref[...] += _dot(z, w2_ref[pl.ds(off, chunk), :])
        return carry

    lax.fori_loop(0, dff // chunk, body, 0)
    o_ref[...] = x + 0.5 * gt_ref[...] * acc_ref[...]


def _ffn(x, shift, scale, gate, ng, w1, w2, *, tm, rows_per_seq):
    n, d = x.shape
    dff = w2.shape[0]
    chunk = 256
    kern = functools.partial(_ffn_kernel, dff=dff, chunk=chunk)
    ms = lambda m: _mod_spec(m, tm, rows_per_seq, d, 1)
    return pl.pallas_call(
        kern,
        out_shape=jax.ShapeDtypeStruct((n, d), F32),
        grid=(n // tm,),
        in_specs=[
            pl.BlockSpec((tm, d), lambda i: (i, 0)),
            ms(shift), ms(scale), ms(gate),
            pl.BlockSpec((1, d), lambda i: (0, 0)),
            pl.BlockSpec((d, 2 * dff), lambda i: (0, 0), pipeline_mode=pl.Buffered(1)),
            pl.BlockSpec((dff, d), lambda i: (0, 0), pipeline_mode=pl.Buffered(1)),
        ],
        out_specs=pl.BlockSpec((tm, d), lambda i: (i, 0)),
        scratch_shapes=[pltpu.VMEM((tm, d), BF16), pltpu.VMEM((tm, d), F32)],
        compiler_params=_cparams(1, 48),
        name="ffn",
    )(x, shift, scale, gate, ng, w1, w2)


def _inproj_kernel(x_ref, sh_ref, sc_ref, ng_ref, w_ref, o_ref, h_ref):
    @pl.when(pl.program_id(1) == 0)
    def _():
        h_ref[...] = _modnorm(x_ref[...], ng_ref[...], sh_ref[...], sc_ref[...]).astype(BF16)

    o_ref[...] = _dot(h_ref[...], w_ref[...])


def _inproj(x, shift, scale, ng, w, *, tm, tn, rows_per_seq):
    n, d = x.shape
    nw = w.shape[1]
    ms = lambda m: _mod_spec(m, tm, rows_per_seq, d, 2)
    return pl.pallas_call(
        _inproj_kernel,
        out_shape=jax.ShapeDtypeStruct((n, nw), F32),
        grid=(n // tm, nw // tn),
        in_specs=[
            pl.BlockSpec((tm, d), lambda i, j: (i, 0)),
            ms(shift), ms(scale),
            pl.BlockSpec((1, d), lambda i, j: (0, 0)),
            pl.BlockSpec((d, tn), lambda i, j: (0, j)),
        ],
        out_specs=pl.BlockSpec((tm, tn), lambda i, j: (i, j)),
        scratch_shapes=[pltpu.VMEM((tm, d), BF16)],
        compiler_params=_cparams(2, 48),
        name="inproj",
    )(x, shift, scale, ng, w)


def _rows_kernel(x_ref, sh_ref, sc_ref, ng_ref, o_ref):
    o_ref[...] = _modnorm(x_ref[...], ng_ref[...], sh_ref[...], sc_ref[...])


def _modnorm_rows(x, shift, scale, ng):
    return pl.pallas_call(
        _rows_kernel, out_shape=jax.ShapeDtypeStruct(x.shape, F32), name="modnorm_rows"
    )(x, shift, scale, ng)


def _mm_kernel(x_ref, w_ref, o_ref):
    o_ref[...] = _dot(x_ref[...].astype(BF16), w_ref[...])


def _matmul_small(x, w):
    return pl.pallas_call(
        _mm_kernel, out_shape=jax.ShapeDtypeStruct((x.shape[0], w.shape[1]), F32), name="mm_small"
    )(x, w)


def _ret_kernel(q_ref, k_ref, v_ref, g_ref, cos_ref, sin_ref, s0_ref, o_ref, s_ref, *, lreal, n_heads, dk, dv):
    assert dk * 2 == LANES and dv == LANES
    lpad = max(lreal, 16)
    half = dk // 2

    @pl.when(pl.program_id(1) == 0)
    def _():
        s_ref[...] = s0_ref[...]

    def rows(x):
        if lpad == lreal:
            return x
        return jnp.concatenate([x, jnp.zeros((lpad - lreal, x.shape[1]), x.dtype)], axis=0)

    cos = rows(cos_ref[...])
    sin = rows(sin_ref[...])
    lane = lax.broadcasted_iota(jnp.int32, (lpad, LANES), 1)
    first = (lane % dk) < half

    def rot(x):
        sw = jnp.where(first, pltpu.roll(x, LANES - half, 1), pltpu.roll(x, half, 1))
        return x * cos + sw * sin

    idx = lax.broadcasted_iota(jnp.int32, (lpad, 1), 0).astype(F32)
    ri = lax.broadcasted_iota(jnp.int32, (lpad, lpad), 0)
    ci = lax.broadcasted_iota(jnp.int32, (lpad, lpad), 1)
    rel = (ri - ci).astype(F32)
    zeros_half = jnp.zeros((dk, dv), F32)

    for p in range(n_heads // 2):
        qp = rot(rows(q_ref[:, p * LANES:(p + 1) * LANES]))
        kp = rot(rows(k_ref[:, p * LANES:(p + 1) * LANES])) * (dk ** -0.5)
        qb = qp.astype(BF16)
        for hh in range(2):
            h = 2 * p + hh
            lg = math.log1p(-(2.0 ** (-5 - h)))
            kh = jnp.where((lane // dk) == hh, kp, 0.0)
            s_old = s_ref[h]
            s_pad = jnp.concatenate([s_old, zeros_half] if hh == 0 else [zeros_half, s_old], axis=0)
            inter = _dot(qb, s_pad.astype(BF16)) * jnp.exp(lg * (idx + 1.0))
            scores = lax.dot_general(qb, kh.astype(BF16), NT_DIMS, preferred_element_type=F32)
            dm = jnp.where(rel >= 0.0, jnp.exp(lg * jnp.maximum(rel, 0.0)), 0.0)
            vb = rows(v_ref[:, h * dv:(h + 1) * dv]).astype(BF16)
            intra = _dot((scores * dm).astype(BF16), vb)
            kd = (kh * jnp.exp(lg * (lreal - 1.0 - idx))).astype(BF16)
            upd = lax.dot_general(kd, vb, TN_DIMS, preferred_element_type=F32)
            s_ref[h] = s_old * math.exp(lg * lreal) + upd[hh * dk:(hh + 1) * dk]
            o = inter + intra
            o = o * lax.rsqrt(jnp.mean(o * o, axis=-1, keepdims=True) + NORM_EPS)
            o_ref[:, h * dv:(h + 1) * dv] = _silu(g_ref[:, h * dv:(h + 1) * dv]) * o[:lreal]


def _retention(p3, cos, sin, s0, *, lreal, cols):
    b, t, _ = p3.shape
    _, n_heads, dk, dv = s0.shape
    wqk, wv = n_heads * dk, n_heads * dv
    kern = functools.partial(_ret_kernel, lreal=lreal, n_heads=n_heads, dk=dk, dv=dv)
    return pl.pallas_call(
        kern,
        out_shape=(jax.ShapeDtypeStruct((b, t, wv), F32), jax.ShapeDtypeStruct(s0.shape, F32)),
        grid=(b, t // lreal),
        in_specs=[
            pl.BlockSpec((None, lreal, wqk), lambda i, c: (i, c, cols["rq"])),
            pl.BlockSpec((None, lreal, wqk), lambda i, c: (i, c, cols["rk"])),
            pl.BlockSpec((None, lreal, wv), lambda i, c: (i, c, cols["rv"])),
            pl.BlockSpec((None, lreal, wv), lambda i, c: (i, c, cols["rg"])),
            pl.BlockSpec((lreal, LANES), lambda i, c: (c, 0)),
            pl.BlockSpec((lreal, LANES), lambda i, c: (c, 0)),
            pl.BlockSpec((None, n_heads, dk, dv), lambda i, c: (i, 0, 0, 0)),
        ],
        out_specs=(
            pl.BlockSpec((None, lreal, wv), lambda i, c: (i, c, 0)),
            pl.BlockSpec((None, n_heads, dk, dv), lambda i, c: (i, 0, 0, 0)),
        ),
        compiler_params=_cparams(2, 32),
        name="retention",
    )(p3, p3, p3, p3, cos, sin, s0)


def _gate_kernel(ff_ref, b_ref, lf_ref, cum_ref, *, t, tb, n_heads):
    bias = b_ref[...]
    ri = lax.broadcasted_iota(jnp.int32, (tb, tb), 0)
    ci = lax.broadcasted_iota(jnp.int32, (tb, tb), 1)
    tri = (ri >= ci).astype(F32)
    carry = jnp.zeros((1, LANES), F32)
    for blk in range(t // tb):
        x = ff_ref[blk * tb:(blk + 1) * tb, :] + bias
        lf = -_softplus(-x)
        cs = jnp.dot(tri, lf, precision=HIGHEST, preferred_element_type=F32) + carry
        carry = cs[tb - 1:tb, :]
        lf_ref[blk * tb:(blk + 1) * tb, :] = lf[:, :n_heads]
        cum_ref[blk * tb:(blk + 1) * tb, :] = cs[:, :n_heads]


def _fox_gate(p3, bias128, *, n_heads, col):
    b, t, _ = p3.shape
    tb = min(t, LANES)
    kern = functools.partial(_gate_kernel, t=t, tb=tb, n_heads=n_heads)
    return pl.pallas_call(
        kern,
        out_shape=(jax.ShapeDtypeStruct((b, t, n_heads), F32), jax.ShapeDtypeStruct((b, t, n_heads), F32)),
        grid=(b,),
        in_specs=[
            pl.BlockSpec((None, t, LANES), lambda i: (i, 0, col)),
            pl.BlockSpec((1, LANES), lambda i: (0, 0)),
        ],
        out_specs=(
            pl.BlockSpec((None, t, n_heads), lambda i: (i, 0, 0)),
            pl.BlockSpec((None, t, n_heads), lambda i: (i, 0, 0)),
        ),
        compiler_params=_cparams(1, 16),
        name="fox_gate",
    )(p3, bias128)


def _foxp_kernel(q_ref, k_ref, v_ref, ck_ref, o_ref, *, tq, tk, hd):
    assert 2 * hd == LANES and tq % tk == 0
    p = pl.program_id(1)
    qi = pl.program_id(2)
    q = q_ref[...] * (hd ** -0.5)
    lane = lax.broadcasted_iota(jnp.int32, (tq, LANES), 1)
    qh = [jnp.where((lane // hd) == hh, q, 0.0).astype(BF16) for hh in range(2)]
    rowpos = qi * tq + lax.broadcasted_iota(jnp.int32, (tq, 1), 0)
    kiota = lax.broadcasted_iota(jnp.int32, (1, tk), 1)

    def body(kb, carry):
        koff = pl.multiple_of(kb * tk, tk)
        kblk = k_ref[pl.ds(koff, tk), :].astype(BF16)
        vblk = v_ref[pl.ds(koff, tk), :].astype(BF16)
        valid = (koff + kiota) <= rowpos
        new = []
        for hh in range(2):
            m, l, acc = carry[hh]
            s = lax.dot_general(qh[hh], kblk, NT_DIMS, preferred_element_type=F32)
            s = s - ck_ref[pl.ds(2 * p + hh, 1), pl.ds(koff, tk)]
            s = jnp.where(valid, s, NEG_BIG)
            m_new = jnp.maximum(m, jnp.max(s, axis=-1, keepdims=True))
            alpha = jnp.exp(m - m_new)
            pr = jnp.exp(s - m_new)
            l = l * alpha + jnp.sum(pr, axis=-1, keepdims=True)
            acc = acc * alpha + _dot(pr.astype(BF16), vblk)
            new.append((m_new, l, acc))
        return tuple(new)

    init = tuple(
        (jnp.full((tq, 1), NEG_BIG, F32), jnp.zeros((tq, 1), F32), jnp.zeros((tq, LANES), F32)) for _ in range(2)
    )
    res = lax.fori_loop(0, (qi + 1) * (tq // tk), body, init)
    o0 = res[0][2] / res[0][1]
    o1 = res[1][2] / res[1][1]
    o_ref[...] = jnp.where(lane < hd, o0, o1)


def _fox_prompt(p3, cum_t, *, n_heads, hd, cols):
    b, t, _ = p3.shape
    tq = min(256, t)
    tk = tq
    n_pairs = n_heads // 2
    kern = functools.partial(_foxp_kernel, tq=tq, tk=tk, hd=hd)
    return pl.pallas_call(
        kern,
        out_shape=jax.ShapeDtypeStruct((b, t, n_heads * hd), F32),
        grid=(b, n_pairs, t // tq),
        in_specs=[
            pl.BlockSpec((None, tq, LANES), lambda i, p, q: (i, q, cols["fq"] + p)),
            pl.BlockSpec((None, t, LANES), lambda i, p, q: (i, 0, cols["fk"] + p)),
            pl.BlockSpec((None, t, LANES), lambda i, p, q: (i, 0, cols["fv"] + p)),
            pl.BlockSpec((None, n_heads, t), lambda i, p, q: (i, 0, 0)),
        ],
        out_specs=pl.BlockSpec((None, tq, LANES), lambda i, p, q: (i, q, p)),
        compiler_params=_cparams(3, 32),
        name="fox_prompt",
    )(p3, p3, p3, cum_t)


def _foxs_kernel(pt_ref, q_ref, kn_ref, vn_ref, c1_ref, *rest, pps, ts, n_heads, hd, page):
    k_refs = rest[:pps]
    v_refs = rest[pps:2 * pps]
    lf_refs = rest[2 * pps:3 * pps]
    o_ref = rest[3 * pps]
    qbd_ref, m_ref, l_ref, acc_ref, car_ref = rest[3 * pps + 1:]
    del pt_ref
    step = pl.program_id(1)
    nrow = n_heads * ts
    w = n_heads * hd
    row = lax.broadcasted_iota(jnp.int32, (nrow, w), 0)
    col = lax.broadcasted_iota(jnp.int32, (nrow, w), 1)
    head_mask = (row // ts) == (col // hd)

    def expand(x):
        return jnp.concatenate([jnp.broadcast_to(x[h:h + 1], (ts, x.shape[1])) for h in range(n_heads)], axis=0)

    def pad_rows(x):
        return jnp.concatenate([x, jnp.zeros((page - ts, x.shape[1]), x.dtype)], axis=0)

    @pl.when(step == 0)
    def _():
        q = q_ref[...] * (hd ** -0.5)
        qbd = jnp.where(head_mask, jnp.concatenate([q] * n_heads, axis=0), 0.0).astype(BF16)
        qbd_ref[...] = qbd
        kn = pad_rows(kn_ref[...]).astype(BF16)
        vn = pad_rows(vn_ref[...]).astype(BF16)
        s = lax.dot_general(qbd, kn, NT_DIMS, preferred_element_type=F32)
        srow = lax.broadcasted_iota(jnp.int32, (nrow, page), 0)
        scol = lax.broadcasted_iota(jnp.int32, (nrow, page), 1)
        s = jnp.where(scol <= (srow % ts), s - c1_ref[...], NEG_BIG)
        m = jnp.max(s, axis=-1, keepdims=True)
        pr = jnp.exp(s - m)
        m_ref[...] = m
        l_ref[...] = jnp.sum(pr, axis=-1, keepdims=True)
        acc_ref[...] = _dot(pr.astype(BF16), vn)
        car_ref[...] = jnp.zeros_like(car_ref)

    qbd = qbd_ref[...]
    ri = lax.broadcasted_iota(jnp.int32, (page, page), 0)
    ci = lax.broadcasted_iota(jnp.int32, (page, page), 1)
    later = (ri > ci).astype(F32)
    for u in range(pps):
        lf = lf_refs[u][...]
        suffix = jnp.dot(lf, later, precision=HIGHEST, preferred_element_type=F32) + car_ref[...]
        s = lax.dot_general(qbd, k_refs[u][...].astype(BF16), NT_DIMS, preferred_element_type=F32)
        s = s + expand(suffix)
        m_old = m_ref[...]
        m_new = jnp.maximum(m_old, jnp.max(s, axis=-1, keepdims=True))
        alpha = jnp.exp(m_old - m_new)
        pr = jnp.exp(s - m_new)
        m_ref[...] = m_new
        l_ref[...] = l_ref[...] * alpha + jnp.sum(pr, axis=-1, keepdims=True)
        acc_ref[...] = acc_ref[...] * alpha + _dot(pr.astype(BF16), v_refs[u][...].astype(BF16))
        car_ref[...] = car_ref[...] + jnp.sum(lf, axis=-1, keepdims=True)

    @pl.when(step == pl.num_programs(1) - 1)
    def _():
        o = jnp.where(head_mask, acc_ref[...] / l_ref[...], 0.0)
        out = o[0:ts]
        for h in range(1, n_heads):
            out = out + o[h * ts:(h + 1) * ts]
        o_ref[...] = out


def _fox_sample(p3, c1, cache_k, cache_v, cache_lf_t, page_table, layer, *, n_heads, hd, cols):
    b, ts, _ = p3.shape
    n_pages = page_table.shape[1]
    page = cache_k.shape[2]
    w = n_heads * hd
    pps = 8
    while n_pages % pps:
        pps //= 2
    nrow = n_heads * ts
    kern = functools.partial(_foxs_kernel, pps=pps, ts=ts, n_heads=n_heads, hd=hd, page=page)

    def page_spec(u, shape):
        def imap(i, s, pt):
            return (layer, pt[i * n_pages + (n_pages - 1 - (s * pps + u))], 0, 0)
        return pl.BlockSpec(shape, imap)

    in_specs = [
        pl.BlockSpec((None, ts, w), lambda i, s, pt: (i, 0, cols["fq"])),
        pl.BlockSpec((None, ts, w), lambda i, s, pt: (i, 0, cols["fk"])),
        pl.BlockSpec((None, ts, w), lambda i, s, pt: (i, 0, cols["fv"])),
        pl.BlockSpec((None, nrow, page), lambda i, s, pt: (i, 0, 0)),
    ]
    in_specs += [page_spec(u, (None, None, page, w)) for u in range(pps)]
    in_specs += [page_spec(u, (None, None, page, w)) for u in range(pps)]
    in_specs += [page_spec(u, (None, None, n_heads, page)) for u in range(pps)]
    grid_spec = pltpu.PrefetchScalarGridSpec(
        num_scalar_prefetch=1,
        grid=(b, n_pages // pps),
        in_specs=in_specs,
        out_specs=pl.BlockSpec((None, ts, w), lambda i, s, pt: (i, 0, 0)),
        scratch_shapes=[
            pltpu.VMEM((nrow, w), BF16),
            pltpu.VMEM((nrow, 1), F32),
            pltpu.VMEM((nrow, 1), F32),
            pltpu.VMEM((nrow, w), F32),
            pltpu.VMEM((n_heads, 1), F32),
        ],
    )
    args = [page_table.reshape(-1), p3, p3, p3, c1] + [cache_k] * pps + [cache_v] * pps + [cache_lf_t] * pps
    return pl.pallas_call(
        kern,
        out_shape=jax.ShapeDtypeStruct((b, ts, w), F32),
        grid_spec=grid_spec,
        compiler_params=_cparams(2, 32),
        name="fox_sample",
    )(*args)


def _rwkv_kernel(rw_ref, prev0_ref, z0_ref, mu_ref, w0_ref, w2_ref, a0_ref, a2_ref, g2_ref, kk_ref, ka_ref,
                 rk_ref, lnw_ref, lnb_ref, obd_ref, o_ref, zo_ref, prev_ref, z_ref, *, tc, c, n_heads, hn):
    assert 2 * hn == LANES and 2 * c == LANES
    cw = n_heads * hn
    n_pairs = n_heads // 2
    ci = pl.program_id(1)

    @pl.when(ci == 0)
    def _():
        prev_ref[...] = prev0_ref[...]
        z_ref[...] = z0_ref[...]

    p_in = rw_ref[...]
    if tc < c:
        p_in = jnp.concatenate([p_in, jnp.zeros((c - tc, p_in.shape[1]), F32)], axis=0)
    rowi = lax.broadcasted_iota(jnp.int32, (c, 1), 0)
    valid = rowi < tc
    p_prev = jnp.where(rowi == 0, prev_ref[...], pltpu.roll(p_in, 1, 0))
    prev_ref[...] = p_in[tc - 1:tc, :]
    pm = p_in + (p_prev - p_in) * mu_ref[...]
    xr = pm[:, 0:cw]
    xk = pm[:, cw:2 * cw]
    xv = jnp.where(valid, pm[:, 2 * cw:3 * cw], 0.0)
    xwa = pm[:, 3 * cw:3 * cw + LANES]
    xg = pm[:, 3 * cw + LANES:]
    obd = obd_ref[...]

    def segsum(x):
        hi = x.astype(BF16)
        lo = (x - hi.astype(F32)).astype(BF16)
        return _dot(hi, obd) + _dot(lo, obd)

    w_raw = -_softplus(-(w0_ref[...] + _dot(jnp.tanh(xwa).astype(BF16), w2_ref[...]))) - 0.5
    lw = jnp.where(valid, -jnp.exp(w_raw), 0.0)
    a_sig = _sigmoid(a0_ref[...] + _dot(xwa.astype(BF16), a2_ref[...]))
    gate = _dot(_sigmoid(xg).astype(BF16), g2_ref[...])
    kk = xk * kk_ref[...]
    kk = kk / jnp.maximum(jnp.sqrt(segsum(kk * kk)), 1e-12)
    kmod = xk * (1.0 + (a_sig - 1.0) * ka_ref[...])
    kk = jnp.where(valid, kk, 0.0)
    kmod_s = jnp.where(valid, kmod, 0.0)
    a_s = -kk
    b_s = kk * a_sig

    ri = lax.broadcasted_iota(jnp.int32, (c, c), 0)
    cj = lax.broadcasted_iota(jnp.int32, (c, c), 1)
    tri = (ri >= cj).astype(F32)
    cum = jnp.dot(tri, lw, precision=HIGHEST, preferred_element_type=F32)
    cum_c = cum[c - 1:c, :]
    g_t = jnp.exp(cum)
    g_inv = jnp.exp(-cum)
    g_rem = jnp.exp(cum_c - cum)
    at = a_s * jnp.exp(cum - lw)
    bt = b_s * g_inv
    kt = kmod_s * g_inv
    rt = xr * g_t
    bh = b_s * g_rem
    kh = kmod_s * g_rem
    g_c = jnp.exp(cum_c)

    r2 = lax.broadcasted_iota(jnp.int32, (LANES, LANES), 0)
    c2 = lax.broadcasted_iota(jnp.int32, (LANES, LANES), 1)
    same_head = (r2 // c) == (c2 // c)
    strict = jnp.logical_and(same_head, r2 > c2)
    incl = jnp.logical_and(same_head, r2 >= c2)
    eye = r2 == c2
    lane_c = lax.broadcasted_iota(jnp.int32, (c, LANES), 1)
    m0 = lane_c < hn
    lane2 = lax.broadcasted_iota(jnp.int32, (LANES, LANES), 1)
    row2 = lax.broadcasted_iota(jnp.int32, (LANES, LANES), 0)
    own = (row2 // c) == (lane2 // hn)

    ys = []
    for p in range(n_pairs):
        sl = slice(p * LANES, (p + 1) * LANES)
        atp, btp, ktp, rtp, bhp, khp, vp = at[:, sl], bt[:, sl], kt[:, sl], rt[:, sl], bh[:, sl], kh[:, sl], xv[:, sl]

        def two(x):
            return jnp.concatenate([jnp.where(m0, x, 0.0), jnp.where(m0, 0.0, x)], axis=0)

        a2 = two(atp).astype(BF16)
        rr2 = two(rtp).astype(BF16)
        b2 = jnp.concatenate([btp, btp], axis=0).astype(BF16)
        k2 = jnp.concatenate([ktp, ktp], axis=0).astype(BF16)
        v2 = jnp.concatenate([vp, vp], axis=0).astype(BF16)
        lab = jnp.where(strict, lax.dot_general(a2, b2, NT_DIMS, preferred_element_type=F32), 0.0)
        lak = jnp.where(strict, lax.dot_general(a2, k2, NT_DIMS, preferred_element_type=F32), 0.0)
        mrb = jnp.where(incl, lax.dot_general(rr2, b2, NT_DIMS, preferred_element_type=F32), 0.0)
        mrk = jnp.where(incl, lax.dot_general(rr2, k2, NT_DIMS, preferred_element_type=F32), 0.0)
        inv = jnp.where(eye, 1.0, lab)
        lp = lab
        n = 1
        while 2 * n < c:
            lpb = lp.astype(BF16)
            lp = _dot(lpb, lpb)
            inv = inv + _dot(inv.astype(BF16), lp.astype(BF16))
            n *= 2
        z = z_ref[p]
        zb = z.astype(BF16)
        rhs = _dot(a2, zb) + _dot(lak.astype(BF16), v2)
        u2 = jnp.where(own, _dot(inv.astype(BF16), rhs.astype(BF16)), 0.0)
        y2 = _dot(rr2, zb) + _dot(mrb.astype(BF16), u2.astype(BF16)) + _dot(mrk.astype(BF16), v2)
        y2 = jnp.where(own, y2, 0.0)
        ys.append(y2[:c] + y2[c:])
        up = u2[:c] + u2[c:]
        lhs_t = jnp.concatenate([bhp, khp], axis=0).astype(BF16)
        rhs_t = jnp.concatenate([up, vp], axis=0).astype(BF16)
        upd = lax.dot_general(lhs_t, rhs_t, TN_DIMS, preferred_element_type=F32)
        gc_col = jnp.sum(jnp.where(eye, jnp.broadcast_to(g_c[:, sl], (LANES, LANES)), 0.0), axis=1, keepdims=True)
        bd = (r2 // hn) == (c2 // hn)
        z_ref[p] = gc_col * z + jnp.where(bd, upd, 0.0)

    y = jnp.concatenate(ys, axis=1)
    rkk = xr * kmod * rk_ref[...]
    sums = segsum(jnp.concatenate([y, rkk], axis=0))
    mean = sums[:c] * (1.0 / hn)
    bonus = sums[c:] * pm[:, 2 * cw:3 * cw]
    dev = y - mean
    var = segsum(dev * dev) * (1.0 / hn)
    yn = dev * lax.rsqrt(var + RWKV_GN_EPS) * lnw_ref[...] + lnb_ref[...]
    out = (yn + bonus) * gate
    o_ref[...] = out[:tc]

    @pl.when(ci == pl.num_programs(1) - 1)
    def _():
        zo_ref[...] = z_ref[...]


def _rwkv(p3, prev0, z0p, params, *, tc, n_heads, hn, col):
    b, t, _ = p3.shape
    c = LANES // 2
    cw = n_heads * hn
    rw_w = prev0.shape[-1]
    n_pairs = n_heads // 2
    kern = functools.partial(_rwkv_kernel, tc=tc, c=c, n_heads=n_heads, hn=hn)
    full = lambda a: pl.BlockSpec(a.shape, lambda i, j: (0,) * a.ndim)
    return pl.pallas_call(
        kern,
        out_shape=(jax.ShapeDtypeStruct((b, t, cw), F32), jax.ShapeDtypeStruct(z0p.shape, F32)),
        grid=(b, t // tc),
        in_specs=[
            pl.BlockSpec((None, tc, rw_w), lambda i, j: (i, j, col)),
            pl.BlockSpec((None, 1, rw_w), lambda i, j: (i, 0, 0)),
            pl.BlockSpec((None, n_pairs, LANES, LANES), lambda i, j: (i, 0, 0, 0)),
        ] + [full(a) for a in params],
        out_specs=(
            pl.BlockSpec((None, tc, cw), lambda i, j: (i, j, 0)),
            pl.BlockSpec((None, n_pairs, LANES, LANES), lambda i, j: (i, 0, 0, 0)),
        ),
        scratch_shapes=[pltpu.VMEM((1, rw_w), F32), pltpu.VMEM((n_pairs, LANES, LANES), F32)],
        compiler_params=_cparams(2, 32),
        name="rwkv",
    )(p3, prev0, z0p, *params)


def _merge_kernel(x_ref, gt_ref, oret_ref, ofox_ref, orw_ref, g0_ref, g1_ref, g2_ref,
                  wr_ref, wf_ref, ww_ref, wo_ref, o_ref):
    merged = _sigmoid(g0_ref[...]) * _dot(oret_ref[...].astype(BF16), wr_ref[...])
    merged += _sigmoid(g1_ref[...]) * _dot(ofox_ref[...].astype(BF16), wf_ref[...])
    merged += _sigmoid(g2_ref[...]) * _dot(orw_ref[...].astype(BF16), ww_ref[...])
    o_ref[...] = x_ref[...] + gt_ref[...] * _dot(merged.astype(BF16), wo_ref[...])


def _merge(x, gate, o_ret, o_fox, o_rwkv, p2, wr, wf, ww, wo, *, tm, rows_per_seq, gcol):
    n, d = x.shape
    row = lambda a: pl.BlockSpec((tm, a.shape[1]), lambda i: (i, 0))
    res = lambda a: pl.BlockSpec(a.shape, lambda i: (0, 0), pipeline_mode=pl.Buffered(1))
    return pl.pallas_call(
        _merge_kernel,
        out_shape=jax.ShapeDtypeStruct((n, d), F32),
        grid=(n // tm,),
        in_specs=[
            row(x), _mod_spec(gate, tm, rows_per_seq, d, 1), row(o_ret), row(o_fox), row(o_rwkv),
            pl.BlockSpec((tm, d), lambda i: (i, gcol)),
            pl.BlockSpec((tm, d), lambda i: (i, gcol + 1)),
            pl.BlockSpec((tm, d), lambda i: (i, gcol + 2)),
            res(wr), res(wf), res(ww), res(wo),
        ],
        out_specs=pl.BlockSpec((tm, d), lambda i: (i, 0)),
        compiler_params=_cparams(1, 48),
        name="merge",
    )(x, gate, o_ret, o_fox, o_rwkv, p2, p2, p2, wr, wf, ww, wo)


def _final_kernel(x_ref, g_ref, o_ref):
    x = x_ref[...]
    o_ref[...] = x * lax.rsqrt(jnp.mean(x * x, axis=-1, keepdims=True) + NORM_EPS) * g_ref[...]


def _final_norm(x, g, *, tm):
    n, d = x.shape
    return pl.pallas_call(
        _final_kernel,
        out_shape=jax.ShapeDtypeStruct((n, d), F32),
        grid=(n // tm,),
        in_specs=[pl.BlockSpec((tm, d), lambda i: (i, 0)), pl.BlockSpec((1, d), lambda i: (0, 0))],
        out_specs=pl.BlockSpec((tm, d), lambda i: (i, 0)),
        compiler_params=_cparams(1, 32),
        name="final_norm",
    )(x, g)


def _largest_tile(n, cap):
    t = min(n, cap)
    while n % t:
        t //= 2
    return t


def kernel(x_prompt, x_sample, cache_fox_k, cache_fox_v, cache_fox_logf, state_ret, state_rwkv_wkv,
           state_rwkv_shift, page_table, c_prompt, c_sample, w_ada, b_ada, norm_g, w_ffn_in, w_ffn_out,
           w_in, b_fox_f, rwkv_mu, rwkv_w0, rwkv_w2, rwkv_a0, rwkv_a2, rwkv_g2, rwkv_k_k, rwkv_k_a,
           rwkv_r_k, rwkv_ln_w, rwkv_ln_b, w_br_ret, w_br_fox, w_br_rwkv, w_o, final_norm_g):
    depth = w_in.shape[0]
    bp, tp, d = x_prompt.shape
    bs, ts, _ = x_sample.shape
    _, _, h_ret, ret_dk, ret_dv = state_ret.shape
    h_fox = b_fox_f.shape[1]
    fox_hd = cache_fox_k.shape[-1]
    _, h_rw, rw_n = rwkv_r_k.shape
    lora_w, lora_a, lora_g = rwkv_w2.shape[1], rwkv_a2.shape[1], rwkv_g2.shape[1]
    rw_c = h_rw * rw_n
    w_qk, w_v, w_fox = h_ret * ret_dk, h_ret * ret_dv, h_fox * fox_hd
    rw_in = 3 * rw_c + lora_w + lora_a + lora_g
    page = cache_fox_k.shape[2]
    n_pool = cache_fox_k.shape[1]
    past_len = page_table.shape[1] * page
    assert lora_w + lora_a == LANES and lora_g == LANES and w_qk == rw_c == w_fox and w_v == d == 2 * w_qk

    src = {}
    off = 0
    for name, size in (("rq", w_qk), ("rk", w_qk), ("rv", w_v), ("rg", w_v), ("fq", w_fox), ("fk", w_fox),
                       ("fv", w_fox), ("ff", h_fox), ("rw", rw_in), ("gates", 3 * d)):
        src[name] = (off, size)
        off += size
    ff_pad = 2 * LANES - h_fox
    order = ("rw", "ff", "rv", "rg", "gates", "rq", "rk", "fq", "fk", "fv")
    dst = {}
    off = 0
    for name in order:
        dst[name] = off
        off += src[name][1] + (ff_pad if name == "ff" else 0)
    nw = off
    cols = {"rq": dst["rq"] // w_qk, "rk": dst["rk"] // w_qk, "rv": dst["rv"] // w_v, "rg": dst["rg"] // w_v}
    fox_cols = {k: dst[k] // LANES for k in ("fq", "fk", "fv")}
    fox_cols_w = {k: dst[k] // w_fox for k in ("fq", "fk", "fv")}
    ff_col = dst["ff"] // LANES
    gcol = dst["gates"] // d
    for name, width in (("rq", w_qk), ("rk", w_qk), ("rv", w_v), ("rg", w_v), ("gates", d), ("fq", w_fox),
                        ("fk", w_fox), ("fv", w_fox), ("ff", LANES), ("rw", rw_in)):
        assert dst[name] % width == 0

    def pack_w_in(w):
        parts = []
        for name in order:
            o, s = src[name]
            parts.append(w[:, o:o + s])
            if name == "ff":
                parts.append(jnp.zeros((d, ff_pad), w.dtype))
        return jnp.concatenate(parts, axis=1).astype(BF16)

    half = ret_dk // 2
    inv = ROPE_BASE ** (-jnp.arange(half, dtype=F32) / half)

    def rope_tables(pos):
        ang = pos[:, None] * inv[None, :]
        cos, sin = jnp.cos(ang), jnp.sin(ang)
        reps = LANES // ret_dk
        return (jnp.tile(jnp.concatenate([cos, cos], axis=1), (1, reps)),
                jnp.tile(jnp.concatenate([-sin, sin], axis=1), (1, reps)))

    cos_p, sin_p = rope_tables(jnp.arange(tp, dtype=F32))
    cos_s, sin_s = rope_tables(jnp.arange(ts, dtype=F32) + past_len)

    m_all = _ada(jnp.concatenate([c_prompt, c_sample], axis=0), w_ada, b_ada)
    m_all = m_all.reshape(depth, bp + bs, 9, d)

    hid = lax.broadcasted_iota(jnp.int32, (rw_c, rw_c), 0) // rw_n
    obd = (hid == hid.T).astype(BF16)
    lf_t_all = jnp.swapaxes(cache_fox_logf, -1, -2)
    ck_pages = cache_fox_k.reshape(depth, n_pool, page, w_fox)
    cv_pages = cache_fox_v.reshape(depth, n_pool, page, w_fox)

    xp = x_prompt.reshape(bp * tp, d)
    xs = x_sample.reshape(bs * ts, d)
    tm_p = _largest_tile(tp, 512)
    outs_p, outs_s = [], []

    for l in range(depth):
        w1 = [w_ffn_in[l, i].astype(BF16) for i in range(2)]
        w2 = [w_ffn_out[l, i].astype(BF16) for i in range(2)]
        wp = pack_w_in(w_in[l])
        wr, wf, ww, wo = (w_br_ret[l].astype(BF16), w_br_fox[l].astype(BF16), w_br_rwkv[l].astype(BF16),
                          w_o[l].astype(BF16))
        zpad_w = jnp.zeros((lora_a, rw_c), F32)
        zpad_a = jnp.zeros((lora_w, rw_c), F32)
        row = lambda v: v.reshape(1, -1)
        rw_params = [
            row(rwkv_mu[l]), row(rwkv_w0[l]),
            jnp.concatenate([rwkv_w2[l], zpad_w], axis=0).astype(BF16), row(rwkv_a0[l]),
            jnp.concatenate([zpad_a, rwkv_a2[l]], axis=0).astype(BF16), rwkv_g2[l].astype(BF16),
            row(rwkv_k_k[l]), row(rwkv_k_a[l]), row(rwkv_r_k[l]), row(rwkv_ln_w[l]), row(rwkv_ln_b[l]), obd,
        ]
        bias128 = jnp.concatenate([b_fox_f[l], jnp.zeros((LANES - h_fox,), F32)]).reshape(1, LANES)
        ng = [norm_g[l, i].reshape(1, d) for i in range(3)]

        for group in ("prompt", "sample"):
            if group == "prompt":
                x, nb, t, m = xp, bp, tp, m_all[l, :bp]
                mods = [m[:, i].reshape(nb, 1, d) for i in range(9)]
                tm, rps = tm_p, tp
            else:
                x, nb, t, m = xs, bs, ts, m_all[l, bp:]
                mods = [jnp.repeat(m[:, i], t, axis=0) for i in range(9)]
                tm, rps = bs * ts, ts
            sh1, sc1, g1, sh2, sc2, g2, sh3, sc3, g3 = mods

            x = _ffn(x, sh1, sc1, g1, ng[0], w1[0], w2[0], tm=tm, rows_per_seq=rps)
            tn = _largest_tile(nw, nw // 4) if (nw // 4) % LANES == 0 else LANES
            tm_in = _largest_tile(t, 1024) if group == "prompt" else tm
            p2 = _inproj(x, sh2, sc2, ng[1], wp, tm=tm_in, tn=tn, rows_per_seq=rps)
            p3 = p2.reshape(nb, t, nw)

            x_last = x.reshape(nb, t, d)[:, -1]
            h_last = _modnorm_rows(x_last, m[:, 3], m[:, 4], ng[1])

            if group == "prompt":
                s0 = jnp.zeros((nb, h_ret, ret_dk, ret_dv), F32)
                cos, sin, lreal = cos_p, sin_p, min(LANES, t)
            else:
                s0 = state_ret[l]
                cos, sin, lreal = cos_s, sin_s, min(LANES, t)
            o_ret, ret_s = _retention(p3, cos, sin, s0, lreal=lreal, cols=cols)

            logf, cum = _fox_gate(p3, bias128, n_heads=h_fox, col=ff_col)
            if group == "prompt":
                o_fox = _fox_prompt(p3, jnp.swapaxes(cum, 1, 2), n_heads=h_fox, hd=fox_hd, cols=fox_cols)
            else:
                cum_t = jnp.swapaxes(cum, 1, 2)
                c1 = jnp.repeat(cum_t, t, axis=1)
                c1 = jnp.pad(c1, ((0, 0), (0, 0), (0, page - t)))
                o_fox = _fox_sample(p3, c1, ck_pages, cv_pages, lf_t_all, page_table, l,
                                    n_heads=h_fox, hd=fox_hd, cols=fox_cols_w)

            if group == "prompt":
                prev0 = jnp.zeros((nb, 1, rw_in), F32)
                s_rw0 = jnp.zeros((nb, h_rw, rw_n, rw_n), F32)
            else:
                prev0 = _matmul_small(state_rwkv_shift[l], wp[:, :rw_in]).reshape(nb, 1, rw_in)
                s_rw0 = state_rwkv_wkv[l]
            zt = jnp.swapaxes(s_rw0, -1, -2).reshape(nb, h_rw // 2, 2, rw_n, rw_n)
            z0p = jnp.zeros((nb, h_rw // 2, 2, rw_n, 2, rw_n), F32)
            z0p = z0p.at[:, :, 0, :, 0, :].set(zt[:, :, 0]).at[:, :, 1, :, 1, :].set(zt[:, :, 1])
            z0p = z0p.reshape(nb, h_rw // 2, 2 * rw_n, 2 * rw_n)
            o_rw, zp = _rwkv(p3, prev0, z0p, rw_params, tc=min(LANES // 2, t), n_heads=h_rw, hn=rw_n, col=0)
            zp = zp.reshape(nb, h_rw // 2, 2, rw_n, 2, rw_n)
            z_new = jnp.stack([zp[:, :, 0, :, 0, :], zp[:, :, 1, :, 1, :]], axis=2)
            rw_s = jnp.swapaxes(z_new.reshape(nb, h_rw, rw_n, rw_n), -1, -2)

            x = _merge(x, g2, o_ret.reshape(nb * t, w_v), o_fox.reshape(nb * t, w_fox), o_rw.reshape(nb * t, rw_c),
                       p2, wr, wf, ww, wo, tm=tm, rows_per_seq=rps, gcol=gcol)
            x = _ffn(x, sh3, sc3, g3, ng[2], w1[1], w2[1], tm=tm, rows_per_seq=rps)

            fox_k = p3[:, :, dst["fk"]:dst["fk"] + w_fox].reshape(nb, t, h_fox, fox_hd)
            fox_v = p3[:, :, dst["fv"]:dst["fv"] + w_fox].reshape(nb, t, h_fox, fox_hd)
            st = (fox_k, fox_v, logf, ret_s, rw_s, h_last)
            if group == "prompt":
                xp = x
                outs_p.append(st)
            else:
                xs = x
                outs_s.append(st)

    fng = final_norm_g.reshape(1, d)
    y_prompt = _final_norm(xp, fng, tm=_largest_tile(bp * tp, 1024)).reshape(bp, tp, d)
    y_sample = _final_norm(xs, fng, tm=bs * ts).reshape(bs, ts, d)
    stack = lambda sts, i: jnp.stack([s[i] for s in sts], axis=0)
    return (y_prompt, y_sample) + tuple(stack(outs_p, i) for i in range(6)) + tuple(stack(outs_s, i) for i in range(6))
```

```python
import functools
import math

import jax
import jax.numpy as jnp
from jax import lax
from jax.experimental import pallas as pl
from jax.experimental.pallas import tpu as pltpu

F32 = jnp.float32
BF16 = jnp.bfloat16
HIGHEST = lax.Precision.HIGHEST

NORM_EPS = 1e-6
RWKV_GN_EPS = 64e-5
ROPE_BASE = 10000.0
NEG_BIG = -1e30
LOG2E = math.log2(math.e)

LANES = 128
MIB = 1 << 20
NT_DIMS = (((1,), (1,)), ((), ()))
TN_DIMS = (((0,), (0,)), ((), ()))


def _cparams(n_grid, vmem_mib):
    return pltpu.CompilerParams(
        dimension_semantics=("arbitrary",) * n_grid,
        vmem_limit_bytes=int(vmem_mib * MIB),
    )


def _dot(a, b):
    return jnp.dot(a, b, preferred_element_type=F32)


def _sigmoid(x):
    return 1.0 / (1.0 + jnp.exp(-x))


def _silu(x):
    return x * _sigmoid(x)


def _softplus(x):
    return jnp.maximum(x, 0.0) + jnp.log1p(jnp.exp(-jnp.abs(x)))


def _modnorm(x, g, shift, scale):
    y = x * lax.rsqrt(jnp.mean(x * x, axis=-1, keepdims=True) + NORM_EPS) * g
    return y * (1.0 + scale) + shift


def _ada_kernel(c_ref, w_ref, b_ref, o_ref):
    s = _silu(c_ref[...]).astype(BF16)
    o_ref[...] = _dot(s, w_ref[...].astype(BF16)) + b_ref[...]


def _ada(c_all, w_ada, b_ada):
    depth, d, n = w_ada.shape
    rows = c_all.shape[0]
    tn = d
    return pl.pallas_call(
        _ada_kernel,
        out_shape=jax.ShapeDtypeStruct((depth, rows, n), F32),
        grid=(depth, n // tn),
        in_specs=[
            pl.BlockSpec((rows, d), lambda l, j: (0, 0)),
            pl.BlockSpec((None, d, tn), lambda l, j: (l, 0, j)),
            pl.BlockSpec((None, 1, tn), lambda l, j: (l, 0, j)),
        ],
        out_specs=pl.BlockSpec((None, rows, tn), lambda l, j: (l, 0, j)),
        compiler_params=_cparams(2, 24),
        name="ada",
    )(c_all, w_ada, b_ada.reshape(depth, 1, n))


def _mod_spec(mod, tm, rows_per_seq, d, n_grid):
    if mod.ndim == 3:
        tiles_per_seq = rows_per_seq // tm
        if n_grid == 1:
            return pl.BlockSpec((None, 1, d), lambda i: (i // tiles_per_seq, 0, 0))
        return pl.BlockSpec((None, 1, d), lambda i, j: (i // tiles_per_seq, 0, 0))
    if n_grid == 1:
        return pl.BlockSpec((tm, d), lambda i: (i, 0))
    return pl.BlockSpec((tm, d), lambda i, j: (i, 0))


def _ffn_kernel(x_ref, sh_ref, sc_ref, gt_ref, ng_ref, w1_ref, w2_ref, o_ref, h_ref, acc_ref, *, dff, chunk):
    x = x_ref[...]
    h_ref[...] = _modnorm(x, ng_ref[...], sh_ref[...], sc_ref[...]).astype(BF16)
    acc_ref[...] = jnp.zeros_like(acc_ref)

    def body(c, carry):
        off = pl.multiple_of(c * chunk, LANES)
        off_b = pl.multiple_of(dff + c * chunk, LANES)
        h = h_ref[...]
        a = _dot(h, w1_ref[:, pl.ds(off, chunk)])
        b = _dot(h, w1_ref[:, pl.ds(off_b, chunk)])
        z = (_silu(a) * b).astype(BF16)
        acc_ref[...] += _dot(z, w2_ref[pl.ds(off, chunk), :])
        return carry

    lax.fori_loop(0, dff // chunk, body, 0)
    o_ref[...] = x + 0.5 * gt_ref[...] * acc_ref[...]


def _ffn(x, shift, scale, gate, ng, w1, w2, *, tm, rows_per_seq):
    n, d = x.shape
    dff = w2.shape[0]
    chunk = 256
    kern = functools.partial(_ffn_kernel, dff=dff, chunk=chunk)
    ms = lambda m: _mod_spec(m, tm, rows_per_seq, d, 1)
    return pl.pallas_call(
        kern,
        out_shape=jax.ShapeDtypeStruct((n, d), F32),
        grid=(n // tm,),
        in_specs=[
            pl.BlockSpec((tm, d), lambda i: (i, 0)),
            ms(shift), ms(scale), ms(gate),
            pl.BlockSpec((1, d), lambda i: (0, 0)),
            pl.BlockSpec((d, 2 * dff), lambda i: (0, 0), pipeline_mode=pl.Buffered(1)),
            pl.BlockSpec((dff, d), lambda i: (0, 0), pipeline_mode=pl.Buffered(1)),
        ],
        out_specs=pl.BlockSpec((tm, d), lambda i: (i, 0)),
        scratch_shapes=[pltpu.VMEM((tm, d), BF16), pltpu.VMEM((tm, d), F32)],
        compiler_params=_cparams(1, 48),
        name="ffn",
    )(x, shift, scale, gate, ng, w1, w2)


def _inproj_kernel(x_ref, sh_ref, sc_ref, ng_ref, w_ref, o_ref, h_ref):
    @pl.when(pl.program_id(1) == 0)
    def _():
        h_ref[...] = _modnorm(x_ref[...], ng_ref[...], sh_ref[...], sc_ref[...]).astype(BF16)

    o_ref[...] = _dot(h_ref[...], w_ref[...])


def _inproj(x, shift, scale, ng, w, *, tm, tn, rows_per_seq):
    n, d = x.shape
    nw = w.shape[1]
    ms = lambda m: _mod_spec(m, tm, rows_per_seq, d, 2)
    return pl.pallas_call(
        _inproj_kernel,
        out_shape=jax.ShapeDtypeStruct((n, nw), F32),
        grid=(n // tm, nw // tn),
        in_specs=[
            pl.BlockSpec((tm, d), lambda i, j: (i, 0)),
            ms(shift), ms(scale),
            pl.BlockSpec((1, d), lambda i, j: (0, 0)),
            pl.BlockSpec((d, tn), lambda i, j: (0, j)),
        ],
        out_specs=pl.BlockSpec((tm, tn), lambda i, j: (i, j)),
        scratch_shapes=[pltpu.VMEM((tm, d), BF16)],
        compiler_params=_cparams(2, 48),
        name="inproj",
    )(x, shift, scale, ng, w)


def _rows_kernel(x_ref, sh_ref, sc_ref, ng_ref, o_ref):
    o_ref[...] = _modnorm(x_ref[...], ng_ref[...], sh_ref[...], sc_ref[...])


def _modnorm_rows(x, shift, scale, ng):
    return pl.pallas_call(
        _rows_kernel, out_shape=jax.ShapeDtypeStruct(x.shape, F32), name="modnorm_rows"
    )(x, shift, scale, ng)


def _mm_kernel(x_ref, w_ref, o_ref):
    o_ref[...] = _dot(x_ref[...].astype(BF16), w_ref[...])


def _matmul_small(x, w):
    return pl.pallas_call(
        _mm_kernel, out_shape=jax.ShapeDtypeStruct((x.shape[0], w.shape[1]), F32), name="mm_small"
    )(x, w)


def _ret_kernel(q_ref, k_ref, v_ref, g_ref, cos_ref, sin_ref, s0_ref, o_ref, s_ref, *, lreal, n_heads, dk, dv):
    assert dk * 2 == LANES and dv == LANES
    lpad = max(lreal, 16)
    half = dk // 2

    @pl.when(pl.program_id(1) == 0)
    def _():
        s_ref[...] = s0_ref[...]

    def rows(x):
        if lpad == lreal:
            return x
        return jnp.concatenate([x, jnp.zeros((lpad - lreal, x.shape[1]), x.dtype)], axis=0)

    cos = rows(cos_ref[...])
    sin = rows(sin_ref[...])
    lane = lax.broadcasted_iota(jnp.int32, (lpad, LANES), 1)
    first = (lane % dk) < half

    def rot(x):
        sw = jnp.where(first, pltpu.roll(x, LANES - half, 1), pltpu.roll(x, half, 1))
        return x * cos + sw * sin

    idx = lax.broadcasted_iota(jnp.int32, (lpad, 1), 0).astype(F32)
    ri = lax.broadcasted_iota(jnp.int32, (lpad, lpad), 0)
    ci = lax.broadcasted_iota(jnp.int32, (lpad, lpad), 1)
    rel = (ri - ci).astype(F32)
    zeros_half = jnp.zeros((dk, dv), F32)

    heads = range(n_heads)
    lgs = [math.log1p(-(2.0 ** (-5 - h))) for h in heads]
    qbs, khs = [], []
    for p in range(n_heads // 2):
        qb = rot(rows(q_ref[:, p * LANES:(p + 1) * LANES])).astype(BF16)
        kp = rot(rows(k_ref[:, p * LANES:(p + 1) * LANES])) * (dk ** -0.5)
        for hh in range(2):
            qbs.append(qb)
            khs.append(jnp.where((lane // dk) == hh, kp, 0.0))
    s_olds = [s_ref[h] for h in heads]
    vbs = [rows(v_ref[:, h * dv:(h + 1) * dv]).astype(BF16) for h in heads]
    inters, scores, upds = [], [], []
    for h in heads:
        s_pad = jnp.concatenate([s_olds[h], zeros_half] if h % 2 == 0 else [zeros_half, s_olds[h]], axis=0)
        inters.append(_dot(qbs[h], s_pad.astype(BF16)))
        scores.append(lax.dot_general(qbs[h], khs[h].astype(BF16), NT_DIMS, preferred_element_type=F32))
        kd = (khs[h] * jnp.exp(lgs[h] * (lreal - 1.0 - idx))).astype(BF16)
        upds.append(lax.dot_general(kd, vbs[h], TN_DIMS, preferred_element_type=F32))
    intras = []
    for h in heads:
        dm = jnp.where(rel >= 0.0, jnp.exp(lgs[h] * jnp.maximum(rel, 0.0)), 0.0)
        intras.append(_dot((scores[h] * dm).astype(BF16), vbs[h]))
    for h in heads:
        hh = h % 2
        s_ref[h] = s_olds[h] * math.exp(lgs[h] * lreal) + upds[h][hh * dk:(hh + 1) * dk]
        o = inters[h] * jnp.exp(lgs[h] * (idx + 1.0)) + intras[h]
        o = o * lax.rsqrt(jnp.mean(o * o, axis=-1, keepdims=True) + NORM_EPS)
        o_ref[:, h * dv:(h + 1) * dv] = _silu(g_ref[:, h * dv:(h + 1) * dv]) * o[:lreal]


def _retention(p3, cos, sin, s0, *, lreal, cols):
    b, t, _ = p3.shape
    _, n_heads, dk, dv = s0.shape
    wqk, wv = n_heads * dk, n_heads * dv
    kern = functools.partial(_ret_kernel, lreal=lreal, n_heads=n_heads, dk=dk, dv=dv)
    return pl.pallas_call(
        kern,
        out_shape=(jax.ShapeDtypeStruct((b, t, wv), F32), jax.ShapeDtypeStruct(s0.shape, F32)),
        grid=(b, t // lreal),
        in_specs=[
            pl.BlockSpec((None, lreal, wqk), lambda i, c: (i, c, cols["rq"])),
            pl.BlockSpec((None, lreal, wqk), lambda i, c: (i, c, cols["rk"])),
            pl.BlockSpec((None, lreal, wv), lambda i, c: (i, c, cols["rv"])),
            pl.BlockSpec((None, lreal, wv), lambda i, c: (i, c, cols["rg"])),
            pl.BlockSpec((lreal, LANES), lambda i, c: (c, 0)),
            pl.BlockSpec((lreal, LANES), lambda i, c: (c, 0)),
            pl.BlockSpec((None, n_heads, dk, dv), lambda i, c: (i, 0, 0, 0)),
        ],
        out_specs=(
            pl.BlockSpec((None, lreal, wv), lambda i, c: (i, c, 0)),
            pl.BlockSpec((None, n_heads, dk, dv), lambda i, c: (i, 0, 0, 0)),
        ),
        compiler_params=_cparams(2, 32),
        name="retention",
    )(p3, p3, p3, p3, cos, sin, s0)


def _gate_kernel(ff_ref, b_ref, lf_ref, cum_ref, *, t, tb, n_heads):
    bias = b_ref[...]
    ri = lax.broadcasted_iota(jnp.int32, (tb, tb), 0)
    ci = lax.broadcasted_iota(jnp.int32, (tb, tb), 1)
    tri = (ri >= ci).astype(F32)
    carry = jnp.zeros((1, LANES), F32)
    for blk in range(t // tb):
        x = ff_ref[blk * tb:(blk + 1) * tb, :] + bias
        lf = -_softplus(-x)
        cs = jnp.dot(tri, lf, precision=HIGHEST, preferred_element_type=F32) + carry
        carry = cs[tb - 1:tb, :]
        lf_ref[blk * tb:(blk + 1) * tb, :] = lf[:, :n_heads]
        cum_ref[blk * tb:(blk + 1) * tb, :] = cs[:, :n_heads]


def _fox_gate(p3, bias128, *, n_heads, col):
    b, t, _ = p3.shape
    tb = min(t, LANES)
    kern = functools.partial(_gate_kernel, t=t, tb=tb, n_heads=n_heads)
    return pl.pallas_call(
        kern,
        out_shape=(jax.ShapeDtypeStruct((b, t, n_heads), F32), jax.ShapeDtypeStruct((b, t, n_heads), F32)),
        grid=(b,),
        in_specs=[
            pl.BlockSpec((None, t, LANES), lambda i: (i, 0, col)),
            pl.BlockSpec((1, LANES), lambda i: (0, 0)),
        ],
        out_specs=(
            pl.BlockSpec((None, t, n_heads), lambda i: (i, 0, 0)),
            pl.BlockSpec((None, t, n_heads), lambda i: (i, 0, 0)),
        ),
        compiler_params=_cparams(1, 16),
        name="fox_gate",
    )(p3, bias128)


def _foxp_kernel(q_ref, k_ref, v_ref, ck_ref, o_ref, *, tq, tk, hd, n_heads):
    assert 2 * hd == LANES and tk % tq == 0
    n_pairs = n_heads // 2
    qi = pl.program_id(1)
    lane = lax.broadcasted_iota(jnp.int32, (tq, LANES), 1)
    first = lane < hd
    q = q_ref[...] * (hd ** -0.5 * LOG2E)
    qh = []
    for p in range(n_pairs):
        qp = q[:, p * LANES:(p + 1) * LANES]
        qh += [jnp.where(first, qp, 0.0).astype(BF16), jnp.where(first, 0.0, qp).astype(BF16)]
    rowpos = qi * tq + lax.broadcasted_iota(jnp.int32, (tq, 1), 0)
    kiota = lax.broadcasted_iota(jnp.int32, (1, tk), 1)
    n_kb = ((qi + 1) * tq + tk - 1) // tk

    def block(kb, carry, masked):
        ms, ls, accs = carry
        koff = pl.multiple_of(kb * tk, tk)
        kblk = k_ref[pl.ds(koff, tk), :].astype(BF16)
        vblk = v_ref[pl.ds(koff, tk), :].astype(BF16)
        ck = ck_ref[:, pl.ds(koff, tk)] * LOG2E
        s_all = [lax.dot_general(qh[h], kblk[:, (h // 2) * LANES:(h // 2 + 1) * LANES], NT_DIMS,
                                 preferred_element_type=F32) for h in range(n_heads)]
        valid = (koff + kiota) <= rowpos
        new_m, new_l, alphas, prs = [], [], [], []
        for h in range(n_heads):
            s = s_all[h] - ck[h:h + 1]
            if masked:
                s = jnp.where(valid, s, NEG_BIG)
            m_new = jnp.maximum(ms[h], jnp.max(s, axis=-1, keepdims=True))
            alpha = jnp.exp2(ms[h] - m_new)
            pr = jnp.exp2(s - m_new)
            new_m.append(m_new)
            new_l.append(ls[h] * alpha + jnp.sum(pr, axis=-1, keepdims=True))
            alphas.append(alpha)
            prs.append(pr.astype(BF16))
        new_acc = []
        for p in range(n_pairs):
            vp = vblk[:, p * LANES:(p + 1) * LANES]
            pv = jnp.where(first, _dot(prs[2 * p], vp), _dot(prs[2 * p + 1], vp))
            new_acc.append(accs[p] * jnp.where(first, alphas[2 * p], alphas[2 * p + 1]) + pv)
        return tuple(new_m), tuple(new_l), tuple(new_acc)

    init = (tuple(jnp.full((tq, 1), NEG_BIG, F32) for _ in range(n_heads)),
            tuple(jnp.zeros((tq, 1), F32) for _ in range(n_heads)),
            tuple(jnp.zeros((tq, LANES), F32) for _ in range(n_pairs)))
    carry = lax.fori_loop(0, n_kb - 1, lambda kb, c: block(kb, c, False), init)
    _, ls, accs = block(n_kb - 1, carry, True)
    for p in range(n_pairs):
        o_ref[:, p * LANES:(p + 1) * LANES] = accs[p] / jnp.where(first, ls[2 * p], ls[2 * p + 1])


def _fox_prompt(p3, cum_t, *, n_heads, hd, cols):
    b, t, _ = p3.shape
    tq = min(256, t)
    tk = min(512, t)
    w = n_heads * hd
    kern = functools.partial(_foxp_kernel, tq=tq, tk=tk, hd=hd, n_heads=n_heads)
    return pl.pallas_call(
        kern,
        out_shape=jax.ShapeDtypeStruct((b, t, w), F32),
        grid=(b, t // tq),
        in_specs=[
            pl.BlockSpec((None, tq, w), lambda i, q: (i, q, cols["fq"])),
            pl.BlockSpec((None, t, w), lambda i, q: (i, 0, cols["fk"])),
            pl.BlockSpec((None, t, w), lambda i, q: (i, 0, cols["fv"])),
            pl.BlockSpec((None, n_heads, t), lambda i, q: (i, 0, 0)),
        ],
        out_specs=pl.BlockSpec((None, tq, w), lambda i, q: (i, q, 0)),
        compiler_params=_cparams(2, 48),
        name="fox_prompt",
    )(p3, p3, p3, cum_t)


def _foxs_kernel(pt_ref, q_ref, kn_ref, vn_ref, c1_ref, *rest, pps, ts, n_heads, hd, page):
    k_refs = rest[:pps]
    v_refs = rest[pps:2 * pps]
    lf_refs = rest[2 * pps:3 * pps]
    o_ref = rest[3 * pps]
    qbd_ref, m_ref, l_ref, acc_ref, car_ref = rest[3 * pps + 1:]
    del pt_ref
    step = pl.program_id(1)
    nrow = n_heads * ts
    w = n_heads * hd
    row = lax.broadcasted_iota(jnp.int32, (nrow, w), 0)
    col = lax.broadcasted_iota(jnp.int32, (nrow, w), 1)
    head_mask = (row % n_heads) == (col // hd)

    def tile_rows(x):
        return jnp.concatenate([x] * ts, axis=0)

    def pad_rows(x):
        return jnp.concatenate([x, jnp.zeros((page - ts, x.shape[1]), x.dtype)], axis=0)

    @pl.when(step == 0)
    def _():
        q = q_ref[...] * (hd ** -0.5)
        q_rep = jnp.concatenate([jnp.broadcast_to(q[t:t + 1], (n_heads, w)) for t in range(ts)], axis=0)
        qbd = jnp.where(head_mask, q_rep, 0.0).astype(BF16)
        qbd_ref[...] = qbd
        kn = pad_rows(kn_ref[...]).astype(BF16)
        vn = pad_rows(vn_ref[...]).astype(BF16)
        s = lax.dot_general(qbd, kn, NT_DIMS, preferred_element_type=F32)
        srow = lax.broadcasted_iota(jnp.int32, (nrow, page), 0)
        scol = lax.broadcasted_iota(jnp.int32, (nrow, page), 1)
        s = jnp.where(scol <= (srow // n_heads), s - tile_rows(c1_ref[...]), NEG_BIG)
        m = jnp.max(s, axis=-1, keepdims=True)
        pr = jnp.exp(s - m)
        m_ref[...] = m
        l_ref[...] = jnp.sum(pr, axis=-1, keepdims=True)
        acc_ref[...] = _dot(pr.astype(BF16), vn)
        car_ref[...] = jnp.zeros_like(car_ref)

    qbd = qbd_ref[...]
    ri = lax.broadcasted_iota(jnp.int32, (page, page), 0)
    ci = lax.broadcasted_iota(jnp.int32, (page, page), 1)
    later = (ri > ci).astype(F32)
    lf_all = jnp.concatenate([lf_refs[u][...] for u in range(pps)], axis=0)
    suffix = jnp.dot(lf_all, later, precision=HIGHEST, preferred_element_type=F32)
    totals = jnp.sum(lf_all, axis=-1, keepdims=True)
    car = car_ref[...]
    ss = []
    for u in range(pps):
        kt = k_refs[u][...].reshape(w, page).astype(BF16)
        suf = suffix[u * n_heads:(u + 1) * n_heads] + car
        ss.append(_dot(qbd, kt) + tile_rows(suf))
        car = car + totals[u * n_heads:(u + 1) * n_heads]
    car_ref[...] = car
    smax = ss[0]
    for u in range(1, pps):
        smax = jnp.maximum(smax, ss[u])
    m_old = m_ref[...]
    m_new = jnp.maximum(m_old, jnp.max(smax, axis=-1, keepdims=True))
    alpha = jnp.exp(m_old - m_new)
    m_ref[...] = m_new
    psum = jnp.zeros((nrow, page), F32)
    acc = acc_ref[...] * alpha
    for u in range(pps):
        pr = jnp.exp(ss[u] - m_new)
        psum = psum + pr
        vt = v_refs[u][...].reshape(w, page).astype(BF16)
        acc = acc + lax.dot_general(pr.astype(BF16), vt, NT_DIMS, preferred_element_type=F32)
    acc_ref[...] = acc
    l_ref[...] = l_ref[...] * alpha + jnp.sum(psum, axis=-1, keepdims=True)

    @pl.when(step == pl.num_programs(1) - 1)
    def _():
        o = jnp.where(head_mask, acc_ref[...] / l_ref[...], 0.0)
        o_ref[...] = jnp.sum(o.reshape(ts, n_heads, w), axis=1)


def _fox_sample(p3, c1, cache_k, cache_v, cache_lf_t, page_table, layer, *, n_heads, hd, cols):
    b, ts, _ = p3.shape
    n_pages = page_table.shape[1]
    page = cache_k.shape[-1]
    w = n_heads * hd
    pps = 16
    while n_pages % pps:
        pps //= 2
    nrow = n_heads * ts
    kern = functools.partial(_foxs_kernel, pps=pps, ts=ts, n_heads=n_heads, hd=hd, page=page)

    def page_spec(u, shape):
        def imap(i, s, pt):
            return (layer, pt[i * n_pages + (n_pages - 1 - (s * pps + u))]) + (0,) * (len(shape) - 2)
        return pl.BlockSpec(shape, imap)

    in_specs = [
        pl.BlockSpec((None, ts, w), lambda i, s, pt: (i, 0, cols["fq"])),
        pl.BlockSpec((None, ts, w), lambda i, s, pt: (i, 0, cols["fk"])),
        pl.BlockSpec((None, ts, w), lambda i, s, pt: (i, 0, cols["fv"])),
        pl.BlockSpec((None, n_heads, page), lambda i, s, pt: (i, 0, 0)),
    ]
    in_specs += [page_spec(u, (None, None, n_heads, hd, page)) for u in range(pps)]
    in_specs += [page_spec(u, (None, None, n_heads, hd, page)) for u in range(pps)]
    in_specs += [page_spec(u, (None, None, n_heads, page)) for u in range(pps)]
    grid_spec = pltpu.PrefetchScalarGridSpec(
        num_scalar_prefetch=1,
        grid=(b, n_pages // pps),
        in_specs=in_specs,
        out_specs=pl.BlockSpec((None, ts, w), lambda i, s, pt: (i, 0, 0)),
        scratch_shapes=[
            pltpu.VMEM((nrow, w), BF16),
            pltpu.VMEM((nrow, 1), F32),
            pltpu.VMEM((nrow, 1), F32),
            pltpu.VMEM((nrow, w), F32),
            pltpu.VMEM((n_heads, 1), F32),
        ],
    )
    args = [page_table.reshape(-1), p3, p3, p3, c1] + [cache_k] * pps + [cache_v] * pps + [cache_lf_t] * pps
    return pl.pallas_call(
        kern,
        out_shape=jax.ShapeDtypeStruct((b, ts, w), F32),
        grid_spec=grid_spec,
        compiler_params=_cparams(2, 48),
        name="fox_sample",
    )(*args)


def _rwkv_kernel(rw_ref, prev0_ref, z0_ref, mu_ref, w0_ref, w2_ref, a0_ref, a2_ref, g2_ref, kk_ref, ka_ref,
                 rk_ref, lnw_ref, lnb_ref, obd_ref, o_ref, zo_ref, prev_ref, z_ref, *, tc, c, n_heads, hn):
    assert 2 * hn == LANES and 2 * c == LANES
    cw = n_heads * hn
    n_pairs = n_heads // 2
    ci = pl.program_id(1)

    @pl.when(ci == 0)
    def _():
        prev_ref[...] = prev0_ref[...]
        z_ref[...] = z0_ref[...]

    p_in = rw_ref[...]
    if tc < c:
        p_in = jnp.concatenate([p_in, jnp.zeros((c - tc, p_in.shape[1]), F32)], axis=0)
    rowi = lax.broadcasted_iota(jnp.int32, (c, 1), 0)
    valid = rowi < tc
    p_prev = jnp.where(rowi == 0, prev_ref[...], pltpu.roll(p_in, 1, 0))
    prev_ref[...] = p_in[tc - 1:tc, :]
    pm = p_in + (p_prev - p_in) * mu_ref[...]
    xr = pm[:, 0:cw]
    xk = pm[:, cw:2 * cw]
    xv = jnp.where(valid, pm[:, 2 * cw:3 * cw], 0.0)
    xwa = pm[:, 3 * cw:3 * cw + LANES]
    xg = pm[:, 3 * cw + LANES:]
    obd = obd_ref[...]

    def segsum(x):
        hi = x.astype(BF16)
        lo = (x - hi.astype(F32)).astype(BF16)
        return _dot(hi, obd) + _dot(lo, obd)

    w_raw = -_softplus(-(w0_ref[...] + _dot(jnp.tanh(xwa).astype(BF16), w2_ref[...]))) - 0.5
    lw = jnp.where(valid, -jnp.exp(w_raw), 0.0)
    a_sig = _sigmoid(a0_ref[...] + _dot(xwa.astype(BF16), a2_ref[...]))
    gate = _dot(_sigmoid(xg).astype(BF16), g2_ref[...])
    kk = xk * kk_ref[...]
    kk = kk / jnp.maximum(jnp.sqrt(segsum(kk * kk)), 1e-12)
    kmod = xk * (1.0 + (a_sig - 1.0) * ka_ref[...])
    kk = jnp.where(valid, kk, 0.0)
    kmod_s = jnp.where(valid, kmod, 0.0)
    a_s = -kk
    b_s = kk * a_sig

    ri = lax.broadcasted_iota(jnp.int32, (c, c), 0)
    cj = lax.broadcasted_iota(jnp.int32, (c, c), 1)
    tri = (ri >= cj).astype(F32)
    cum = jnp.dot(tri, lw, precision=HIGHEST, preferred_element_type=F32)
    cum_c = cum[c - 1:c, :]
    g_t = jnp.exp(cum)
    g_inv = jnp.exp(-cum)
    g_rem = jnp.exp(cum_c - cum)
    at = a_s * jnp.exp(cum - lw)
    bt = b_s * g_inv
    kt = kmod_s * g_inv
    rt = xr * g_t
    bh = b_s * g_rem
    kh = kmod_s * g_rem
    g_c = jnp.exp(cum_c)

    r2 = lax.broadcasted_iota(jnp.int32, (LANES, LANES), 0)
    c2 = lax.broadcasted_iota(jnp.int32, (LANES, LANES), 1)
    same_head = (r2 // c) == (c2 // c)
    strict = jnp.logical_and(same_head, r2 > c2)
    incl = jnp.logical_and(same_head, r2 >= c2)
    eye = r2 == c2
    lane_c = lax.broadcasted_iota(jnp.int32, (c, LANES), 1)
    m0 = lane_c < hn
    lane2 = lax.broadcasted_iota(jnp.int32, (LANES, LANES), 1)
    row2 = lax.broadcasted_iota(jnp.int32, (LANES, LANES), 0)
    own = (row2 // c) == (lane2 // hn)

    def two(x):
        return jnp.concatenate([jnp.where(m0, x, 0.0), jnp.where(m0, 0.0, x)], axis=0)

    def twice(x):
        return jnp.concatenate([x, x], axis=0)

    def nt(a, b):
        return lax.dot_general(a, b, NT_DIMS, preferred_element_type=F32)

    pairs = range(n_pairs)
    sls = [slice(p * LANES, (p + 1) * LANES) for p in pairs]
    a2 = [two(at[:, s]).astype(BF16) for s in sls]
    rr2 = [two(rt[:, s]).astype(BF16) for s in sls]
    b2 = [twice(bt[:, s]).astype(BF16) for s in sls]
    k2 = [twice(kt[:, s]).astype(BF16) for s in sls]
    v2 = [twice(xv[:, s]).astype(BF16) for s in sls]
    zs = [z_ref[p] for p in pairs]
    zb = [z.astype(BF16) for z in zs]
    az = [_dot(a2[p], zb[p]) for p in pairs]
    rz = [_dot(rr2[p], zb[p]) for p in pairs]
    lab = [jnp.where(strict, nt(a2[p], b2[p]), 0.0) for p in pairs]
    lak = [jnp.where(strict, nt(a2[p], k2[p]), 0.0).astype(BF16) for p in pairs]
    mrb = [jnp.where(incl, nt(rr2[p], b2[p]), 0.0).astype(BF16) for p in pairs]
    mrk = [jnp.where(incl, nt(rr2[p], k2[p]), 0.0).astype(BF16) for p in pairs]
    rhs = [az[p] + _dot(lak[p], v2[p]) for p in pairs]
    ykv = [rz[p] + _dot(mrk[p], v2[p]) for p in pairs]
    inv = [jnp.where(eye, 1.0, lab[p]) for p in pairs]
    lp = lab
    n = 1
    while 2 * n < c:
        lpb = [x.astype(BF16) for x in lp]
        lp = [_dot(x, x) for x in lpb]
        inv = [inv[p] + _dot(inv[p].astype(BF16), lp[p].astype(BF16)) for p in pairs]
        n *= 2
    u2 = [jnp.where(own, _dot(inv[p].astype(BF16), rhs[p].astype(BF16)), 0.0) for p in pairs]
    y2 = [jnp.where(own, ykv[p] + _dot(mrb[p], u2[p].astype(BF16)), 0.0) for p in pairs]
    bd = (r2 // hn) == (c2 // hn)
    for p in pairs:
        up = u2[p][:c] + u2[p][c:]
        lhs_t = jnp.concatenate([bh[:, sls[p]], kh[:, sls[p]]], axis=0).astype(BF16)
        rhs_t = jnp.concatenate([up, xv[:, sls[p]]], axis=0).astype(BF16)
        upd = lax.dot_general(lhs_t, rhs_t, TN_DIMS, preferred_element_type=F32)
        gc_col = jnp.sum(jnp.where(eye, jnp.broadcast_to(g_c[:, sls[p]], (LANES, LANES)), 0.0),
                         axis=1, keepdims=True)
        z_ref[p] = gc_col * zs[p] + jnp.where(bd, upd, 0.0)

    y = jnp.concatenate([y2[p][:c] + y2[p][c:] for p in pairs], axis=1)
    rkk = xr * kmod * rk_ref[...]
    sums = segsum(jnp.concatenate([y, rkk], axis=0))
    mean = sums[:c] * (1.0 / hn)
    bonus = sums[c:] * pm[:, 2 * cw:3 * cw]
    dev = y - mean
    var = segsum(dev * dev) * (1.0 / hn)
    yn = dev * lax.rsqrt(var + RWKV_GN_EPS) * lnw_ref[...] + lnb_ref[...]
    out = (yn + bonus) * gate
    o_ref[...] = out[:tc]

    @pl.when(ci == pl.num_programs(1) - 1)
    def _():
        zo_ref[...] = z_ref[...]


def _rwkv(p3, prev0, z0p, params, *, tc, n_heads, hn, col):
    b, t, _ = p3.shape
    c = LANES // 2
    cw = n_heads * hn
    rw_w = prev0.shape[-1]
    n_pairs = n_heads // 2
    kern = functools.partial(_rwkv_kernel, tc=tc, c=c, n_heads=n_heads, hn=hn)
    full = lambda a: pl.BlockSpec(a.shape, lambda i, j: (0,) * a.ndim)
    return pl.pallas_call(
        kern,
        out_shape=(jax.ShapeDtypeStruct((b, t, cw), F32), jax.ShapeDtypeStruct(z0p.shape, F32)),
        grid=(b, t // tc),
        in_specs=[
            pl.BlockSpec((None, tc, rw_w), lambda i, j: (i, j, col)),
            pl.BlockSpec((None, 1, rw_w), lambda i, j: (i, 0, 0)),
            pl.BlockSpec((None, n_pairs, LANES, LANES), lambda i, j: (i, 0, 0, 0)),
        ] + [full(a) for a in params],
        out_specs=(
            pl.BlockSpec((None, tc, cw), lambda i, j: (i, j, 0)),
            pl.BlockSpec((None, n_pairs, LANES, LANES), lambda i, j: (i, 0, 0, 0)),
        ),
        scratch_shapes=[pltpu.VMEM((1, rw_w), F32), pltpu.VMEM((n_pairs, LANES, LANES), F32)],
        compiler_params=_cparams(2, 32),
        name="rwkv",
    )(p3, prev0, z0p, *params)


def _merge_kernel(x_ref, gt_ref, oret_ref, ofox_ref, orw_ref, g0_ref, g1_ref, g2_ref,
                  wr_ref, wf_ref, ww_ref, wo_ref, o_ref):
    merged = _sigmoid(g0_ref[...]) * _dot(oret_ref[...].astype(BF16), wr_ref[...])
    merged += _sigmoid(g1_ref[...]) * _dot(ofox_ref[...].astype(BF16), wf_ref[...])
    merged += _sigmoid(g2_ref[...]) * _dot(orw_ref[...].astype(BF16), ww_ref[...])
    o_ref[...] = x_ref[...] + gt_ref[...] * _dot(merged.astype(BF16), wo_ref[...])


def _merge(x, gate, o_ret, o_fox, o_rwkv, p2, wr, wf, ww, wo, *, tm, rows_per_seq, gcol):
    n, d = x.shape
    row = lambda a: pl.BlockSpec((tm, a.shape[1]), lambda i: (i, 0))
    res = lambda a: pl.BlockSpec(a.shape, lambda i: (0, 0), pipeline_mode=pl.Buffered(1))
    return pl.pallas_call(
        _merge_kernel,
        out_shape=jax.ShapeDtypeStruct((n, d), F32),
        grid=(n // tm,),
        in_specs=[
            row(x), _mod_spec(gate, tm, rows_per_seq, d, 1), row(o_ret), row(o_fox), row(o_rwkv),
            pl.BlockSpec((tm, d), lambda i: (i, gcol)),
            pl.BlockSpec((tm, d), lambda i: (i, gcol + 1)),
            pl.BlockSpec((tm, d), lambda i: (i, gcol + 2)),
            res(wr), res(wf), res(ww), res(wo),
        ],
        out_specs=pl.BlockSpec((tm, d), lambda i: (i, 0)),
        compiler_params=_cparams(1, 48),
        name="merge",
    )(x, gate, o_ret, o_fox, o_rwkv, p2, p2, p2, wr, wf, ww, wo)


def _final_kernel(x_ref, g_ref, o_ref):
    x = x_ref[...]
    o_ref[...] = x * lax.rsqrt(jnp.mean(x * x, axis=-1, keepdims=True) + NORM_EPS) * g_ref[...]


def _final_norm(x, g, *, tm):
    n, d = x.shape
    return pl.pallas_call(
        _final_kernel,
        out_shape=jax.ShapeDtypeStruct((n, d), F32),
        grid=(n // tm,),
        in_specs=[pl.BlockSpec((tm, d), lambda i: (i, 0)), pl.BlockSpec((1, d), lambda i: (0, 0))],
        out_specs=pl.BlockSpec((tm, d), lambda i: (i, 0)),
        compiler_params=_cparams(1, 32),
        name="final_norm",
    )(x, g)


def _largest_tile(n, cap):
    t = min(n, cap)
    while n % t:
        t //= 2
    return t


def kernel(x_prompt, x_sample, cache_fox_k, cache_fox_v, cache_fox_logf, state_ret, state_rwkv_wkv,
           state_rwkv_shift, page_table, c_prompt, c_sample, w_ada, b_ada, norm_g, w_ffn_in, w_ffn_out,
           w_in, b_fox_f, rwkv_mu, rwkv_w0, rwkv_w2, rwkv_a0, rwkv_a2, rwkv_g2, rwkv_k_k, rwkv_k_a,
           rwkv_r_k, rwkv_ln_w, rwkv_ln_b, w_br_ret, w_br_fox, w_br_rwkv, w_o, final_norm_g):
    depth = w_in.shape[0]
    bp, tp, d = x_prompt.shape
    bs, ts, _ = x_sample.shape
    _, _, h_ret, ret_dk, ret_dv = state_ret.shape
    h_fox = b_fox_f.shape[1]
    fox_hd = cache_fox_k.shape[-1]
    _, h_rw, rw_n = rwkv_r_k.shape
    lora_w, lora_a, lora_g = rwkv_w2.shape[1], rwkv_a2.shape[1], rwkv_g2.shape[1]
    rw_c = h_rw * rw_n
    w_qk, w_v, w_fox = h_ret * ret_dk, h_ret * ret_dv, h_fox * fox_hd
    rw_in = 3 * rw_c + lora_w + lora_a + lora_g
    page = cache_fox_k.shape[2]
    n_pool = cache_fox_k.shape[1]
    past_len = page_table.shape[1] * page
    assert lora_w + lora_a == LANES and lora_g == LANES and w_qk == rw_c == w_fox and w_v == d == 2 * w_qk

    src = {}
    off = 0
    for name, size in (("rq", w_qk), ("rk", w_qk), ("rv", w_v), ("rg", w_v), ("fq", w_fox), ("fk", w_fox),
                       ("fv", w_fox), ("ff", h_fox), ("rw", rw_in), ("gates", 3 * d)):
        src[name] = (off, size)
        off += size
    ff_pad = 2 * LANES - h_fox
    order = ("rw", "ff", "rv", "rg", "gates", "rq", "rk", "fq", "fk", "fv")
    dst = {}
    off = 0
    for name in order:
        dst[name] = off
        off += src[name][1] + (ff_pad if name == "ff" else 0)
    nw = off
    cols = {"rq": dst["rq"] // w_qk, "rk": dst["rk"] // w_qk, "rv": dst["rv"] // w_v, "rg": dst["rg"] // w_v}
    fox_cols = {k: dst[k] // LANES for k in ("fq", "fk", "fv")}
    fox_cols_w = {k: dst[k] // w_fox for k in ("fq", "fk", "fv")}
    ff_col = dst["ff"] // LANES
    gcol = dst["gates"] // d
    for name, width in (("rq", w_qk), ("rk", w_qk), ("rv", w_v), ("rg", w_v), ("gates", d), ("fq", w_fox),
                        ("fk", w_fox), ("fv", w_fox), ("ff", LANES), ("rw", rw_in)):
        assert dst[name] % width == 0

    def pack_w_in(w):
        parts = []
        for name in order:
            o, s = src[name]
            parts.append(w[:, o:o + s])
            if name == "ff":
                parts.append(jnp.zeros((d, ff_pad), w.dtype))
        return jnp.concatenate(parts, axis=1).astype(BF16)

    half = ret_dk // 2
    inv = ROPE_BASE ** (-jnp.arange(half, dtype=F32) / half)

    def rope_tables(pos):
        ang = pos[:, None] * inv[None, :]
        cos, sin = jnp.cos(ang), jnp.sin(ang)
        reps = LANES // ret_dk
        return (jnp.tile(jnp.concatenate([cos, cos], axis=1), (1, reps)),
                jnp.tile(jnp.concatenate([-sin, sin], axis=1), (1, reps)))

    cos_p, sin_p = rope_tables(jnp.arange(tp, dtype=F32))
    cos_s, sin_s = rope_tables(jnp.arange(ts, dtype=F32) + past_len)

    m_all = _ada(jnp.concatenate([c_prompt, c_sample], axis=0), w_ada, b_ada)
    m_all = m_all.reshape(depth, bp + bs, 9, d)

    hid = lax.broadcasted_iota(jnp.int32, (rw_c, rw_c), 0) // rw_n
    obd = (hid == hid.T).astype(BF16)
    lf_t_all = jnp.swapaxes(cache_fox_logf, -1, -2)
    ck_t = jnp.transpose(cache_fox_k, (0, 1, 3, 4, 2))
    cv_t = jnp.transpose(cache_fox_v, (0, 1, 3, 4, 2))

    xp = x_prompt.reshape(bp * tp, d)
    xs = x_sample.reshape(bs * ts, d)
    tm_p = _largest_tile(tp, 512)
    outs_p, outs_s = [], []

    for l in range(depth):
        w1 = [w_ffn_in[l, i].astype(BF16) for i in range(2)]
        w2 = [w_ffn_out[l, i].astype(BF16) for i in range(2)]
        wp = pack_w_in(w_in[l])
        wr, wf, ww, wo = (w_br_ret[l].astype(BF16), w_br_fox[l].astype(BF16), w_br_rwkv[l].astype(BF16),
                          w_o[l].astype(BF16))
        zpad_w = jnp.zeros((lora_a, rw_c), F32)
        zpad_a = jnp.zeros((lora_w, rw_c), F32)
        row = lambda v: v.reshape(1, -1)
        rw_params = [
            row(rwkv_mu[l]), row(rwkv_w0[l]),
            jnp.concatenate([rwkv_w2[l], zpad_w], axis=0).astype(BF16), row(rwkv_a0[l]),
            jnp.concatenate([zpad_a, rwkv_a2[l]], axis=0).astype(BF16), rwkv_g2[l].astype(BF16),
            row(rwkv_k_k[l]), row(rwkv_k_a[l]), row(rwkv_r_k[l]), row(rwkv_ln_w[l]), row(rwkv_ln_b[l]), obd,
        ]
        bias128 = jnp.concatenate([b_fox_f[l], jnp.zeros((LANES - h_fox,), F32)]).reshape(1, LANES)
        ng = [norm_g[l, i].reshape(1, d) for i in range(3)]

        for group in ("prompt", "sample"):
            if group == "prompt":
                x, nb, t, m = xp, bp, tp, m_all[l, :bp]
                mods = [m[:, i].reshape(nb, 1, d) for i in range(9)]
                tm, rps = tm_p, tp
            else:
                x, nb, t, m = xs, bs, ts, m_all[l, bp:]
                mods = [jnp.repeat(m[:, i], t, axis=0) for i in range(9)]
                tm, rps = bs * ts, ts
            sh1, sc1, g1, sh2, sc2, g2, sh3, sc3, g3 = mods

            x = _ffn(x, sh1, sc1, g1, ng[0], w1[0], w2[0], tm=tm, rows_per_seq=rps)
            tn = _largest_tile(nw, nw // 4) if (nw // 4) % LANES == 0 else LANES
            tm_in = _largest_tile(t, 1024) if group == "prompt" else tm
            p2 = _inproj(x, sh2, sc2, ng[1], wp, tm=tm_in, tn=tn, rows_per_seq=rps)
            p3 = p2.reshape(nb, t, nw)

            x_last = x.reshape(nb, t, d)[:, -1]
            h_last = _modnorm_rows(x_last, m[:, 3], m[:, 4], ng[1])

            if group == "prompt":
                s0 = jnp.zeros((nb, h_ret, ret_dk, ret_dv), F32)
                cos, sin, lreal = cos_p, sin_p, min(LANES, t)
            else:
                s0 = state_ret[l]
                cos, sin, lreal = cos_s, sin_s, min(LANES, t)
            o_ret, ret_s = _retention(p3, cos, sin, s0, lreal=lreal, cols=cols)

            logf, cum = _fox_gate(p3, bias128, n_heads=h_fox, col=ff_col)
            if group == "prompt":
                o_fox = _fox_prompt(p3, jnp.swapaxes(cum, 1, 2), n_heads=h_fox, hd=fox_hd, cols=fox_cols_w)
            else:
                c1 = jnp.pad(jnp.swapaxes(cum, 1, 2), ((0, 0), (0, 0), (0, page - t)))
                o_fox = _fox_sample(p3, c1, ck_t, cv_t, lf_t_all, page_table, l,
                                    n_heads=h_fox, hd=fox_hd, cols=fox_cols_w)

            z0p = jnp.zeros((nb, h_rw // 2, 2, rw_n, 2, rw_n), F32)
            if group == "prompt":
                prev0 = jnp.zeros((nb, 1, rw_in), F32)
            else:
                prev0 = _matmul_small(state_rwkv_shift[l], wp[:, :rw_in]).reshape(nb, 1, rw_in)
                zt = jnp.swapaxes(state_rwkv_wkv[l], -1, -2).reshape(nb, h_rw // 2, 2, rw_n, rw_n)
                z0p = z0p.at[:, :, 0, :, 0, :].set(zt[:, :, 0]).at[:, :, 1, :, 1, :].set(zt[:, :, 1])
            z0p = z0p.reshape(nb, h_rw // 2, 2 * rw_n, 2 * rw_n)
            o_rw, zp = _rwkv(p3, prev0, z0p, rw_params, tc=min(LANES // 2, t), n_heads=h_rw, hn=rw_n, col=0)
            zp = zp.reshape(nb, h_rw // 2, 2, rw_n, 2, rw_n)
            z_new = jnp.stack([zp[:, :, 0, :, 0, :], zp[:, :, 1, :, 1, :]], axis=2)
            rw_s = jnp.swapaxes(z_new.reshape(nb, h_rw, rw_n, rw_n), -1, -2)

            x = _merge(x, g2, o_ret.reshape(nb * t, w_v), o_fox.reshape(nb * t, w_fox), o_rw.reshape(nb * t, rw_c),
                       p2, wr, wf, ww, wo, tm=tm, rows_per_seq=rps, gcol=gcol)
            x = _ffn(x, sh3, sc3, g3, ng[2], w1[1], w2[1], tm=tm, rows_per_seq=rps)

            fox_k = p3[:, :, dst["fk"]:dst["fk"] + w_fox].reshape(nb, t, h_fox, fox_hd)
            fox_v = p3[:, :, dst["fv"]:dst["fv"] + w_fox].reshape(nb, t, h_fox, fox_hd)
            st = (fox_k, fox_v, logf, ret_s, rw_s, h_last)
            if group == "prompt":
                xp = x
                outs_p.append(st)
            else:
                xs = x
                outs_s.append(st)

    fng = final_norm_g.reshape(1, d)
    y_prompt = _final_norm(xp, fng, tm=_largest_tile(bp * tp, 1024)).reshape(bp, tp, d)
    y_sample = _final_norm(xs, fng, tm=bs * ts).reshape(bs, ts, d)
    stack = lambda sts, i: jnp.stack([s[i] for s in sts], axis=0)
    return (y_prompt, y_sample) + tuple(stack(outs_p, i) for i in range(6)) + tuple(stack(outs_s, i) for i in range(6))
```

```python
import functools
import math

import jax
import jax.numpy as jnp
from jax import lax
from jax.experimental import pallas as pl
from jax.experimental.pallas import tpu as pltpu

F32 = jnp.float32
BF16 = jnp.bfloat16
HIGHEST = lax.Precision.HIGHEST

NORM_EPS = 1e-6
RWKV_GN_EPS = 64e-5
ROPE_BASE = 10000.0
NEG_BIG = -1e30
LOG2E = math.log2(math.e)

LANES = 128
MIB = 1 << 20
NT_DIMS = (((1,), (1,)), ((), ()))
TN_DIMS = (((0,), (0,)), ((), ()))


def _cparams(n_grid, vmem_mib):
    return pltpu.CompilerParams(
        dimension_semantics=("arbitrary",) * n_grid,
        vmem_limit_bytes=int(vmem_mib * MIB),
    )


def _dot(a, b):
    return jnp.dot(a, b, preferred_element_type=F32)


def _sigmoid(x):
    return 1.0 / (1.0 + jnp.exp(-x))


def _silu(x):
    return x * _sigmoid(x)


def _softplus(x):
    return jnp.maximum(x, 0.0) + jnp.log1p(jnp.exp(-jnp.abs(x)))


def _modnorm(x, g, shift, scale):
    y = x * lax.rsqrt(jnp.mean(x * x, axis=-1, keepdims=True) + NORM_EPS) * g
    return y * (1.0 + scale) + shift


def _ada_kernel(c_ref, w_ref, b_ref, o_ref):
    s = _silu(c_ref[...]).astype(BF16)
    o_ref[...] = _dot(s, w_ref[...].astype(BF16)) + b_ref[...]


def _ada(c_all, w_ada, b_ada):
    depth, d, n = w_ada.shape
    rows = c_all.shape[0]
    tn = d
    return pl.pallas_call(
        _ada_kernel,
        out_shape=jax.ShapeDtypeStruct((depth, rows, n), F32),
        grid=(depth, n // tn),
        in_specs=[
            pl.BlockSpec((rows, d), lambda l, j: (0, 0)),
            pl.BlockSpec((None, d, tn), lambda l, j: (l, 0, j)),
            pl.BlockSpec((None, 1, tn), lambda l, j: (l, 0, j)),
        ],
        out_specs=pl.BlockSpec((None, rows, tn), lambda l, j: (l, 0, j)),
        compiler_params=_cparams(2, 24),
        name="ada",
    )(c_all, w_ada, b_ada.reshape(depth, 1, n))


def _mod_spec(mod, tm, rows_per_seq, d, n_grid):
    if mod.ndim == 3:
        tiles_per_seq = rows_per_seq // tm
        if n_grid == 1:
            return pl.BlockSpec((None, 1, d), lambda i: (i // tiles_per_seq, 0, 0))
        return pl.BlockSpec((None, 1, d), lambda i, j: (i // tiles_per_seq, 0, 0))
    if n_grid == 1:
        return pl.BlockSpec((tm, d), lambda i: (i, 0))
    return pl.BlockSpec((tm, d), lambda i, j: (i, 0))


def _ffn_kernel(x_ref, sh_ref, sc_ref, gt_ref, ng_ref, w1_ref, w2_ref, o_ref, h_ref, acc_ref, *, dff, chunk):
    x = x_ref[...]
    h_ref[...] = _modnorm(x, ng_ref[...], sh_ref[...], sc_ref[...]).astype(BF16)
    acc_ref[...] = jnp.zeros_like(acc_ref)

    def body(c, carry):
        off = pl.multiple_of(c * chunk, LANES)
        off_b = pl.multiple_of(dff + c * chunk, LANES)
        h = h_ref[...]
        a = _dot(h, w1_ref[:, pl.ds(off, chunk)])
        b = _dot(h, w1_ref[:, pl.ds(off_b, chunk)])
        z = (_silu(a) * b).astype(BF16)
        acc_ref[...] += _dot(z, w2_ref[pl.ds(off, chunk), :])
        return carry

    lax.fori_loop(0, dff // chunk, body, 0)
    o_ref[...] = x + 0.5 * gt_ref[...] * acc_ref[...]


def _ffn(x, shift, scale, gate, ng, w1, w2, *, tm, rows_per_seq):
    n, d = x.shape
    dff = w2.shape[0]
    chunk = 256
    kern = functools.partial(_ffn_kernel, dff=dff, chunk=chunk)
    ms = lambda m: _mod_spec(m, tm, rows_per_seq, d, 1)
    return pl.pallas_call(
        kern,
        out_shape=jax.ShapeDtypeStruct((n, d), F32),
        grid=(n // tm,),
        in_specs=[
            pl.BlockSpec((tm, d), lambda i: (i, 0)),
            ms(shift), ms(scale), ms(gate),
            pl.BlockSpec((1, d), lambda i: (0, 0)),
            pl.BlockSpec((d, 2 * dff), lambda i: (0, 0), pipeline_mode=pl.Buffered(1)),
            pl.BlockSpec((dff, d), lambda i: (0, 0), pipeline_mode=pl.Buffered(1)),
        ],
        out_specs=pl.BlockSpec((tm, d), lambda i: (i, 0)),
        scratch_shapes=[pltpu.VMEM((tm, d), BF16), pltpu.VMEM((tm, d), F32)],
        compiler_params=_cparams(1, 48),
        name="ffn",
    )(x, shift, scale, gate, ng, w1, w2)


def _inproj_kernel(x_ref, sh_ref, sc_ref, ng_ref, w_ref, o_ref, h_ref):
    @pl.when(pl.program_id(1) == 0)
    def _():
        h_ref[...] = _modnorm(x_ref[...], ng_ref[...], sh_ref[...], sc_ref[...]).astype(BF16)

    o_ref[...] = _dot(h_ref[...], w_ref[...])


def _inproj(x, shift, scale, ng, w, *, tm, tn, rows_per_seq):
    n, d = x.shape
    nw = w.shape[1]
    ms = lambda m: _mod_spec(m, tm, rows_per_seq, d, 2)
    return pl.pallas_call(
        _inproj_kernel,
        out_shape=jax.ShapeDtypeStruct((n, nw), F32),
        grid=(n // tm, nw // tn),
        in_specs=[
            pl.BlockSpec((tm, d), lambda i, j: (i, 0)),
            ms(shift), ms(scale),
            pl.BlockSpec((1, d), lambda i, j: (0, 0)),
            pl.BlockSpec((d, tn), lambda i, j: (0, j)),
        ],
        out_specs=pl.BlockSpec((tm, tn), lambda i, j: (i, j)),
        scratch_shapes=[pltpu.VMEM((tm, d), BF16)],
        compiler_params=_cparams(2, 48),
        name="inproj",
    )(x, shift, scale, ng, w)


def _rows_kernel(x_ref, sh_ref, sc_ref, ng_ref, o_ref):
    o_ref[...] = _modnorm(x_ref[...], ng_ref[...], sh_ref[...], sc_ref[...])


def _modnorm_rows(x, shift, scale, ng):
    return pl.pallas_call(
        _rows_kernel, out_shape=jax.ShapeDtypeStruct(x.shape, F32), name="modnorm_rows"
    )(x, shift, scale, ng)


def _mm_kernel(x_ref, w_ref, o_ref):
    o_ref[...] = _dot(x_ref[...].astype(BF16), w_ref[...])


def _matmul_small(x, w):
    return pl.pallas_call(
        _mm_kernel, out_shape=jax.ShapeDtypeStruct((x.shape[0], w.shape[1]), F32), name="mm_small"
    )(x, w)


def _ret_kernel(q_ref, k_ref, v_ref, g_ref, cos_ref, sin_ref, s0_ref, o_ref, s_ref, *, lreal, n_heads, dk, dv):
    assert dk * 2 == LANES and dv == LANES
    lpad = max(lreal, 16)
    half = dk // 2

    @pl.when(pl.program_id(1) == 0)
    def _():
        s_ref[...] = s0_ref[...]

    def rows(x):
        if lpad == lreal:
            return x
        return jnp.concatenate([x, jnp.zeros((lpad - lreal, x.shape[1]), x.dtype)], axis=0)

    cos = rows(cos_ref[...])
    sin = rows(sin_ref[...])
    lane = lax.broadcasted_iota(jnp.int32, (lpad, LANES), 1)
    first = (lane % dk) < half

    def rot(x):
        sw = jnp.where(first, pltpu.roll(x, LANES - half, 1), pltpu.roll(x, half, 1))
        return x * cos + sw * sin

    idx = lax.broadcasted_iota(jnp.int32, (lpad, 1), 0).astype(F32)
    zeros_half = jnp.zeros((dk, dv), F32)
    heads = range(n_heads)
    lgs = [math.log1p(-(2.0 ** (-5 - h))) for h in heads]
    ri = lax.broadcasted_iota(jnp.int32, (lpad, lpad), 0)
    ci = lax.broadcasted_iota(jnp.int32, (lpad, lpad), 1)
    rel = (ri - ci).astype(F32)

    qbs, khs = [], []
    for p in range(n_heads // 2):
        qb = rot(rows(q_ref[:, p * LANES:(p + 1) * LANES])).astype(BF16)
        kp = rot(rows(k_ref[:, p * LANES:(p + 1) * LANES])) * (dk ** -0.5)
        for hh in range(2):
            qbs.append(qb)
            khs.append(jnp.where((lane // dk) == hh, kp, 0.0))
    s_olds = [s_ref[h] for h in heads]
    vbs = [rows(v_ref[:, h * dv:(h + 1) * dv]).astype(BF16) for h in heads]
    inters, scores, upds = [], [], []
    for h in heads:
        s_pad = jnp.concatenate([s_olds[h], zeros_half] if h % 2 == 0 else [zeros_half, s_olds[h]], axis=0)
        inters.append(_dot(qbs[h], s_pad.astype(BF16)))
        scores.append(lax.dot_general(qbs[h], khs[h].astype(BF16), NT_DIMS, preferred_element_type=F32))
        kd = (khs[h] * jnp.exp(lgs[h] * (lreal - 1.0 - idx))).astype(BF16)
        upds.append(lax.dot_general(kd, vbs[h], TN_DIMS, preferred_element_type=F32))
    intras = []
    for h in heads:
        dm = jnp.where(rel >= 0.0, jnp.exp(lgs[h] * jnp.maximum(rel, 0.0)), 0.0)
        intras.append(_dot((scores[h] * dm).astype(BF16), vbs[h]))
    for h in heads:
        hh = h % 2
        s_ref[h] = s_olds[h] * math.exp(lgs[h] * lreal) + upds[h][hh * dk:(hh + 1) * dk]
        o = inters[h] * jnp.exp(lgs[h] * (idx + 1.0)) + intras[h]
        o = o * lax.rsqrt(jnp.mean(o * o, axis=-1, keepdims=True) + NORM_EPS)
        o_ref[:, h * dv:(h + 1) * dv] = _silu(g_ref[:, h * dv:(h + 1) * dv]) * o[:lreal]


def _retention(p3, cos, sin, s0, *, lreal, cols):
    b, t, _ = p3.shape
    _, n_heads, dk, dv = s0.shape
    wqk, wv = n_heads * dk, n_heads * dv
    kern = functools.partial(_ret_kernel, lreal=lreal, n_heads=n_heads, dk=dk, dv=dv)
    return pl.pallas_call(
        kern,
        out_shape=(jax.ShapeDtypeStruct((b, t, wv), F32), jax.ShapeDtypeStruct(s0.shape, F32)),
        grid=(b, t // lreal),
        in_specs=[
            pl.BlockSpec((None, lreal, wqk), lambda i, c: (i, c, cols["rq"])),
            pl.BlockSpec((None, lreal, wqk), lambda i, c: (i, c, cols["rk"])),
            pl.BlockSpec((None, lreal, wv), lambda i, c: (i, c, cols["rv"])),
            pl.BlockSpec((None, lreal, wv), lambda i, c: (i, c, cols["rg"])),
            pl.BlockSpec((lreal, LANES), lambda i, c: (c, 0)),
            pl.BlockSpec((lreal, LANES), lambda i, c: (c, 0)),
            pl.BlockSpec((None, n_heads, dk, dv), lambda i, c: (i, 0, 0, 0)),
        ],
        out_specs=(
            pl.BlockSpec((None, lreal, wv), lambda i, c: (i, c, 0)),
            pl.BlockSpec((None, n_heads, dk, dv), lambda i, c: (i, 0, 0, 0)),
        ),
        compiler_params=_cparams(2, 32),
        name="retention",
    )(p3, p3, p3, p3, cos, sin, s0)


def _gate_kernel(ff_ref, b_ref, lf_ref, cum_ref, *, t, tb, n_heads):
    bias = b_ref[...]
    ri = lax.broadcasted_iota(jnp.int32, (tb, tb), 0)
    ci = lax.broadcasted_iota(jnp.int32, (tb, tb), 1)
    tri = (ri >= ci).astype(F32)
    carry = jnp.zeros((1, LANES), F32)
    for blk in range(t // tb):
        x = ff_ref[blk * tb:(blk + 1) * tb, :] + bias
        lf = -_softplus(-x)
        cs = jnp.dot(tri, lf, precision=HIGHEST, preferred_element_type=F32) + carry
        carry = cs[tb - 1:tb, :]
        lf_ref[blk * tb:(blk + 1) * tb, :] = lf[:, :n_heads]
        cum_ref[blk * tb:(blk + 1) * tb, :] = cs[:, :n_heads]


def _fox_gate(p3, bias128, *, n_heads, col):
    b, t, _ = p3.shape
    tb = min(t, LANES)
    kern = functools.partial(_gate_kernel, t=t, tb=tb, n_heads=n_heads)
    return pl.pallas_call(
        kern,
        out_shape=(jax.ShapeDtypeStruct((b, t, n_heads), F32), jax.ShapeDtypeStruct((b, t, n_heads), F32)),
        grid=(b,),
        in_specs=[
            pl.BlockSpec((None, t, LANES), lambda i: (i, 0, col)),
            pl.BlockSpec((1, LANES), lambda i: (0, 0)),
        ],
        out_specs=(
            pl.BlockSpec((None, t, n_heads), lambda i: (i, 0, 0)),
            pl.BlockSpec((None, t, n_heads), lambda i: (i, 0, 0)),
        ),
        compiler_params=_cparams(1, 16),
        name="fox_gate",
    )(p3, bias128)


def _foxp_kernel(q_ref, k_ref, v_ref, ck_ref, o_ref, *, tq, tk, hd, n_heads):
    assert 2 * hd == LANES and tk % tq == 0
    n_pairs = n_heads // 2
    qi = pl.program_id(1)
    lane = lax.broadcasted_iota(jnp.int32, (tq, LANES), 1)
    first = lane < hd
    q = q_ref[...] * (hd ** -0.5 * LOG2E)
    qh = []
    for p in range(n_pairs):
        qp = q[:, p * LANES:(p + 1) * LANES]
        qh += [jnp.where(first, qp, 0.0).astype(BF16), jnp.where(first, 0.0, qp).astype(BF16)]
    rowpos = qi * tq + lax.broadcasted_iota(jnp.int32, (tq, 1), 0)
    kiota = lax.broadcasted_iota(jnp.int32, (1, tk), 1)
    n_kb = ((qi + 1) * tq + tk - 1) // tk

    def block(kb, carry, masked):
        ms, ls, accs = carry
        koff = pl.multiple_of(kb * tk, tk)
        kblk = k_ref[pl.ds(koff, tk), :].astype(BF16)
        vblk = v_ref[pl.ds(koff, tk), :].astype(BF16)
        ck = ck_ref[:, pl.ds(koff, tk)] * LOG2E
        s_all = [lax.dot_general(qh[h], kblk[:, (h // 2) * LANES:(h // 2 + 1) * LANES], NT_DIMS,
                                 preferred_element_type=F32) for h in range(n_heads)]
        valid = (koff + kiota) <= rowpos
        new_m, new_l, alphas, prs = [], [], [], []
        for h in range(n_heads):
            s = s_all[h] - ck[h:h + 1]
            if masked:
                s = jnp.where(valid, s, NEG_BIG)
            m_new = jnp.maximum(ms[h], jnp.max(s, axis=-1, keepdims=True))
            alpha = jnp.exp2(ms[h] - m_new)
            pr = jnp.exp2(s - m_new)
            new_m.append(m_new)
            new_l.append(ls[h] * alpha + jnp.sum(pr, axis=-1, keepdims=True))
            alphas.append(alpha)
            prs.append(pr.astype(BF16))
        new_acc = []
        for p in range(n_pairs):
            vp = vblk[:, p * LANES:(p + 1) * LANES]
            pv = jnp.where(first, _dot(prs[2 * p], vp), _dot(prs[2 * p + 1], vp))
            new_acc.append(accs[p] * jnp.where(first, alphas[2 * p], alphas[2 * p + 1]) + pv)
        return tuple(new_m), tuple(new_l), tuple(new_acc)

    init = (tuple(jnp.full((tq, 1), NEG_BIG, F32) for _ in range(n_heads)),
            tuple(jnp.zeros((tq, 1), F32) for _ in range(n_heads)),
            tuple(jnp.zeros((tq, LANES), F32) for _ in range(n_pairs)))
    carry = lax.fori_loop(0, n_kb - 1, lambda kb, c: block(kb, c, False), init)
    _, ls, accs = block(n_kb - 1, carry, True)
    for p in range(n_pairs):
        o_ref[:, p * LANES:(p + 1) * LANES] = accs[p] / jnp.where(first, ls[2 * p], ls[2 * p + 1])


def _fox_prompt(p3, cum_t, *, n_heads, hd, cols):
    b, t, _ = p3.shape
    tq = min(256, t)
    tk = min(512, t)
    w = n_heads * hd
    kern = functools.partial(_foxp_kernel, tq=tq, tk=tk, hd=hd, n_heads=n_heads)
    return pl.pallas_call(
        kern,
        out_shape=jax.ShapeDtypeStruct((b, t, w), F32),
        grid=(b, t // tq),
        in_specs=[
            pl.BlockSpec((None, tq, w), lambda i, q: (i, q, cols["fq"])),
            pl.BlockSpec((None, t, w), lambda i, q: (i, 0, cols["fk"])),
            pl.BlockSpec((None, t, w), lambda i, q: (i, 0, cols["fv"])),
            pl.BlockSpec((None, n_heads, t), lambda i, q: (i, 0, 0)),
        ],
        out_specs=pl.BlockSpec((None, tq, w), lambda i, q: (i, q, 0)),
        compiler_params=_cparams(2, 48),
        name="fox_prompt",
    )(p3, p3, p3, cum_t)


def _foxs_kernel(pt_ref, q_ref, kn_ref, vn_ref, c1_ref, *rest, pps, ts, n_heads, hd, page):
    k_refs = rest[:pps]
    v_refs = rest[pps:2 * pps]
    lf_refs = rest[2 * pps:3 * pps]
    o_ref = rest[3 * pps]
    qbd_ref, m_ref, l_ref, acc_ref, car_ref = rest[3 * pps + 1:]
    del pt_ref
    step = pl.program_id(1)
    nrow = n_heads * ts
    w = n_heads * hd
    row = lax.broadcasted_iota(jnp.int32, (nrow, w), 0)
    col = lax.broadcasted_iota(jnp.int32, (nrow, w), 1)
    head_mask = (row % n_heads) == (col // hd)

    def tile_rows(x):
        return jnp.concatenate([x] * ts, axis=0)

    def pad_rows(x):
        return jnp.concatenate([x, jnp.zeros((page - ts, x.shape[1]), x.dtype)], axis=0)

    @pl.when(step == 0)
    def _():
        q = q_ref[...] * (hd ** -0.5)
        q_rep = jnp.concatenate([jnp.broadcast_to(q[t:t + 1], (n_heads, w)) for t in range(ts)], axis=0)
        qbd = jnp.where(head_mask, q_rep, 0.0).astype(BF16)
        qbd_ref[...] = qbd
        kn = pad_rows(kn_ref[...]).astype(BF16)
        vn = pad_rows(vn_ref[...]).astype(BF16)
        s = lax.dot_general(qbd, kn, NT_DIMS, preferred_element_type=F32)
        srow = lax.broadcasted_iota(jnp.int32, (nrow, page), 0)
        scol = lax.broadcasted_iota(jnp.int32, (nrow, page), 1)
        s = jnp.where(scol <= (srow // n_heads), s - tile_rows(c1_ref[...]), NEG_BIG)
        m = jnp.max(s, axis=-1, keepdims=True)
        pr = jnp.exp(s - m)
        m_ref[...] = m
        l_ref[...] = jnp.sum(pr, axis=-1, keepdims=True)
        acc_ref[...] = _dot(pr.astype(BF16), vn)
        car_ref[...] = jnp.zeros_like(car_ref)

    qbd = qbd_ref[...]
    ri = lax.broadcasted_iota(jnp.int32, (page, page), 0)
    ci = lax.broadcasted_iota(jnp.int32, (page, page), 1)
    later = (ri > ci).astype(F32)
    lf_all = jnp.concatenate([lf_refs[u][...] for u in range(pps)], axis=0)
    suffix = jnp.dot(lf_all, later, precision=HIGHEST, preferred_element_type=F32)
    totals = jnp.sum(lf_all, axis=-1, keepdims=True)
    car = car_ref[...]
    ss = []
    for u in range(pps):
        kt = k_refs[u][...].reshape(w, page).astype(BF16)
        suf = suffix[u * n_heads:(u + 1) * n_heads] + car
        ss.append(_dot(qbd, kt) + tile_rows(suf))
        car = car + totals[u * n_heads:(u + 1) * n_heads]
    car_ref[...] = car
    smax = ss[0]
    for u in range(1, pps):
        smax = jnp.maximum(smax, ss[u])
    m_old = m_ref[...]
    m_new = jnp.maximum(m_old, jnp.max(smax, axis=-1, keepdims=True))
    alpha = jnp.exp(m_old - m_new)
    m_ref[...] = m_new
    psum = jnp.zeros((nrow, page), F32)
    acc = acc_ref[...] * alpha
    for u in range(pps):
        pr = jnp.exp(ss[u] - m_new)
        psum = psum + pr
        vt = v_refs[u][...].reshape(w, page).astype(BF16)
        acc = acc + lax.dot_general(pr.astype(BF16), vt, NT_DIMS, preferred_element_type=F32)
    acc_ref[...] = acc
    l_ref[...] = l_ref[...] * alpha + jnp.sum(psum, axis=-1, keepdims=True)

    @pl.when(step == pl.num_programs(1) - 1)
    def _():
        o = jnp.where(head_mask, acc_ref[...] / l_ref[...], 0.0)
        o_ref[...] = jnp.sum(o.reshape(ts, n_heads, w), axis=1)


def _fox_sample(p3, c1, cache_k, cache_v, cache_lf_t, page_table, layer, *, n_heads, hd, cols):
    b, ts, _ = p3.shape
    n_pages = page_table.shape[1]
    page = cache_k.shape[-1]
    w = n_heads * hd
    pps = 16
    while n_pages % pps:
        pps //= 2
    nrow = n_heads * ts
    kern = functools.partial(_foxs_kernel, pps=pps, ts=ts, n_heads=n_heads, hd=hd, page=page)

    def page_spec(u, shape):
        def imap(i, s, pt):
            return (layer, pt[i * n_pages + (n_pages - 1 - (s * pps + u))]) + (0,) * (len(shape) - 2)
        return pl.BlockSpec(shape, imap)

    in_specs = [
        pl.BlockSpec((None, ts, w), lambda i, s, pt: (i, 0, cols["fq"])),
        pl.BlockSpec((None, ts, w), lambda i, s, pt: (i, 0, cols["fk"])),
        pl.BlockSpec((None, ts, w), lambda i, s, pt: (i, 0, cols["fv"])),
        pl.BlockSpec((None, n_heads, page), lambda i, s, pt: (i, 0, 0)),
    ]
    in_specs += [page_spec(u, (None, None, n_heads, hd, page)) for u in range(pps)]
    in_specs += [page_spec(u, (None, None, n_heads, hd, page)) for u in range(pps)]
    in_specs += [page_spec(u, (None, None, n_heads, page)) for u in range(pps)]
    grid_spec = pltpu.PrefetchScalarGridSpec(
        num_scalar_prefetch=1,
        grid=(b, n_pages // pps),
        in_specs=in_specs,
        out_specs=pl.BlockSpec((None, ts, w), lambda i, s, pt: (i, 0, 0)),
        scratch_shapes=[
            pltpu.VMEM((nrow, w), BF16),
            pltpu.VMEM((nrow, 1), F32),
            pltpu.VMEM((nrow, 1), F32),
            pltpu.VMEM((nrow, w), F32),
            pltpu.VMEM((n_heads, 1), F32),
        ],
    )
    args = [page_table.reshape(-1), p3, p3, p3, c1] + [cache_k] * pps + [cache_v] * pps + [cache_lf_t] * pps
    return pl.pallas_call(
        kern,
        out_shape=jax.ShapeDtypeStruct((b, ts, w), F32),
        grid_spec=grid_spec,
        compiler_params=_cparams(2, 48),
        name="fox_sample",
    )(*args)


def _rwkv_kernel(rw_ref, prev0_ref, z0_ref, mu_ref, w0_ref, w2_ref, a0_ref, a2_ref, g2_ref, kk_ref, ka_ref,
                 rk_ref, lnw_ref, lnb_ref, obd_ref, o_ref, zo_ref, prev_ref, z_ref, *, nseq, tc, c, n_heads, hn):
    assert 2 * hn == LANES and 2 * c == LANES
    cw = n_heads * hn
    n_pairs = n_heads // 2
    rows = nseq * c
    ci = pl.program_id(1)

    @pl.when(ci == 0)
    def _():
        prev_ref[...] = prev0_ref[...]
        z_ref[...] = z0_ref[...]

    def pad_chunk(x):
        if tc == c:
            return x
        return jnp.concatenate([x, jnp.zeros((c - tc, x.shape[1]), F32)], axis=0)

    p_in = jnp.concatenate([pad_chunk(rw_ref[i]) for i in range(nseq)], axis=0)
    rowi = lax.broadcasted_iota(jnp.int32, (rows, 1), 0)
    valid = (rowi % c) < tc
    p_prev = pltpu.roll(p_in, 1, 0)
    for i in range(nseq):
        p_prev = jnp.where(rowi == i * c, prev_ref[i], p_prev)
        prev_ref[i] = p_in[i * c + tc - 1:i * c + tc, :]
    pm = p_in + (p_prev - p_in) * mu_ref[...]
    xr = pm[:, 0:cw]
    xk = pm[:, cw:2 * cw]
    xv = jnp.where(valid, pm[:, 2 * cw:3 * cw], 0.0)
    xwa = pm[:, 3 * cw:3 * cw + LANES]
    xg = pm[:, 3 * cw + LANES:]
    obd = obd_ref[...]

    def segsum(x):
        hi = x.astype(BF16)
        lo = (x - hi.astype(F32)).astype(BF16)
        return _dot(hi, obd) + _dot(lo, obd)

    w_raw = -_softplus(-(w0_ref[...] + _dot(jnp.tanh(xwa).astype(BF16), w2_ref[...]))) - 0.5
    lw = jnp.where(valid, -jnp.exp(w_raw), 0.0)
    a_sig = _sigmoid(a0_ref[...] + _dot(xwa.astype(BF16), a2_ref[...]))
    gate = _dot(_sigmoid(xg).astype(BF16), g2_ref[...])
    kk = xk * kk_ref[...]
    kk = kk / jnp.maximum(jnp.sqrt(segsum(kk * kk)), 1e-12)
    kmod = xk * (1.0 + (a_sig - 1.0) * ka_ref[...])
    kk = jnp.where(valid, kk, 0.0)
    kmod_s = jnp.where(valid, kmod, 0.0)
    a_s = -kk
    b_s = kk * a_sig

    ri = lax.broadcasted_iota(jnp.int32, (rows, rows), 0)
    cj = lax.broadcasted_iota(jnp.int32, (rows, rows), 1)
    tri = jnp.logical_and(ri >= cj, (ri // c) == (cj // c)).astype(F32)
    cum = jnp.dot(tri, lw, precision=HIGHEST, preferred_element_type=F32)
    cum_ends = [cum[i * c + c - 1:i * c + c, :] for i in range(nseq)]
    cum_c = jnp.concatenate([jnp.broadcast_to(e, (c, cw)) for e in cum_ends], axis=0)
    g_t = jnp.exp(cum)
    g_inv = jnp.exp(-cum)
    g_rem = jnp.exp(cum_c - cum)
    at = a_s * jnp.exp(cum - lw)
    bt = b_s * g_inv
    kt = kmod_s * g_inv
    rt = xr * g_t
    bh = b_s * g_rem
    kh = kmod_s * g_rem
    g_cs = [jnp.exp(e) for e in cum_ends]

    r2 = lax.broadcasted_iota(jnp.int32, (LANES, LANES), 0)
    c2 = lax.broadcasted_iota(jnp.int32, (LANES, LANES), 1)
    same_head = (r2 // c) == (c2 // c)
    strict = jnp.logical_and(same_head, r2 > c2)
    incl = jnp.logical_and(same_head, r2 >= c2)
    eye = r2 == c2
    lane_c = lax.broadcasted_iota(jnp.int32, (c, LANES), 1)
    m0 = lane_c < hn
    lane2 = lax.broadcasted_iota(jnp.int32, (LANES, LANES), 1)
    row2 = lax.broadcasted_iota(jnp.int32, (LANES, LANES), 0)
    own = (row2 // c) == (lane2 // hn)

    def two(x):
        return jnp.concatenate([jnp.where(m0, x, 0.0), jnp.where(m0, 0.0, x)], axis=0)

    def twice(x):
        return jnp.concatenate([x, x], axis=0)

    def nt(a, b):
        return lax.dot_general(a, b, NT_DIMS, preferred_element_type=F32)

    units = [(i, p) for i in range(nseq) for p in range(n_pairs)]
    nu = range(len(units))

    def blk(x, i, p):
        return x[i * c:(i + 1) * c, p * LANES:(p + 1) * LANES]

    ar = [jnp.concatenate([two(blk(at, i, p)), two(blk(rt, i, p))], axis=0).astype(BF16) for i, p in units]
    bk = [jnp.concatenate([twice(blk(bt, i, p)), twice(blk(kt, i, p))], axis=0).astype(BF16) for i, p in units]
    v2 = [twice(blk(xv, i, p)).astype(BF16) for i, p in units]
    zs = [z_ref[i, p] for i, p in units]
    zmm = [_dot(ar[u], zs[u].astype(BF16)) for u in nu]
    gm = [nt(ar[u], bk[u]) for u in nu]
    lab = [jnp.where(strict, g[:LANES, :LANES], 0.0) for g in gm]
    lm = [jnp.concatenate([jnp.where(strict, g[:LANES, LANES:], 0.0),
                           jnp.where(incl, g[LANES:, LANES:], 0.0)], axis=0).astype(BF16) for g in gm]
    mrb = [jnp.where(incl, g[LANES:, :LANES], 0.0).astype(BF16) for g in gm]
    vmm = [zmm[u] + _dot(lm[u], v2[u]) for u in nu]
    s_k = [jnp.where(eye, 1.0, x) for x in lab]
    p_k = [_dot(x.astype(BF16), x.astype(BF16)) for x in lab]
    span = 4
    while span < c:
        sp = [_dot(jnp.concatenate([s_k[u], p_k[u]], axis=0).astype(BF16), p_k[u].astype(BF16)) for u in nu]
        s_k = [s_k[u] + sp[u][:LANES] for u in nu]
        p_k = [x[LANES:] for x in sp]
        span *= 2
    inv = [s_k[u] + _dot(s_k[u].astype(BF16), p_k[u].astype(BF16)) for u in nu]
    u2 = [jnp.where(own, _dot(inv[u].astype(BF16), vmm[u][:LANES].astype(BF16)), 0.0) for u in nu]
    y2 = [jnp.where(own, vmm[u][LANES:] + _dot(mrb[u], u2[u].astype(BF16)), 0.0) for u in nu]
    bd = (r2 // hn) == (c2 // hn)
    for u, (i, p) in enumerate(units):
        up = u2[u][:c] + u2[u][c:]
        lhs_t = jnp.concatenate([blk(bh, i, p), blk(kh, i, p)], axis=0).astype(BF16)
        rhs_t = jnp.concatenate([up, blk(xv, i, p)], axis=0).astype(BF16)
        upd = lax.dot_general(lhs_t, rhs_t, TN_DIMS, preferred_element_type=F32)
        g_c = g_cs[i][:, p * LANES:(p + 1) * LANES]
        gc_col = jnp.sum(jnp.where(eye, jnp.broadcast_to(g_c, (LANES, LANES)), 0.0), axis=1, keepdims=True)
        z_ref[i, p] = gc_col * zs[u] + jnp.where(bd, upd, 0.0)

    y = jnp.concatenate(
        [jnp.concatenate([y2[i * n_pairs + p][:c] + y2[i * n_pairs + p][c:] for p in range(n_pairs)], axis=1)
         for i in range(nseq)], axis=0)
    rkk = xr * kmod * rk_ref[...]
    sums = segsum(jnp.concatenate([y, rkk], axis=0))
    mean = sums[:rows] * (1.0 / hn)
    bonus = sums[rows:] * pm[:, 2 * cw:3 * cw]
    dev = y - mean
    var = segsum(dev * dev) * (1.0 / hn)
    yn = dev * lax.rsqrt(var + RWKV_GN_EPS) * lnw_ref[...] + lnb_ref[...]
    out = (yn + bonus) * gate
    for i in range(nseq):
        o_ref[i] = out[i * c:i * c + tc]

    @pl.when(ci == pl.num_programs(1) - 1)
    def _():
        zo_ref[...] = z_ref[...]


def _rwkv(p3, prev0, z0p, params, *, tc, n_heads, hn, col):
    b, t, _ = p3.shape
    c = LANES // 2
    cw = n_heads * hn
    rw_w = prev0.shape[-1]
    n_pairs = n_heads // 2
    nseq = 2 if b % 2 == 0 else 1
    kern = functools.partial(_rwkv_kernel, nseq=nseq, tc=tc, c=c, n_heads=n_heads, hn=hn)
    full = lambda a: pl.BlockSpec(a.shape, lambda i, j: (0,) * a.ndim)
    return pl.pallas_call(
        kern,
        out_shape=(jax.ShapeDtypeStruct((b, t, cw), F32), jax.ShapeDtypeStruct(z0p.shape, F32)),
        grid=(b // nseq, t // tc),
        in_specs=[
            pl.BlockSpec((nseq, tc, rw_w), lambda i, j: (i, j, col)),
            pl.BlockSpec((nseq, 1, rw_w), lambda i, j: (i, 0, 0)),
            pl.BlockSpec((nseq, n_pairs, LANES, LANES), lambda i, j: (i, 0, 0, 0)),
        ] + [full(a) for a in params],
        out_specs=(
            pl.BlockSpec((nseq, tc, cw), lambda i, j: (i, j, 0)),
            pl.BlockSpec((nseq, n_pairs, LANES, LANES), lambda i, j: (i, 0, 0, 0)),
        ),
        scratch_shapes=[pltpu.VMEM((nseq, 1, rw_w), F32), pltpu.VMEM((nseq, n_pairs, LANES, LANES), F32)],
        compiler_params=_cparams(2, 32),
        name="rwkv",
    )(p3, prev0, z0p, *params)


def _merge_kernel(x_ref, gt_ref, oret_ref, ofox_ref, orw_ref, g0_ref, g1_ref, g2_ref,
                  wr_ref, wf_ref, ww_ref, wo_ref, o_ref):
    merged = _sigmoid(g0_ref[...]) * _dot(oret_ref[...].astype(BF16), wr_ref[...])
    merged += _sigmoid(g1_ref[...]) * _dot(ofox_ref[...].astype(BF16), wf_ref[...])
    merged += _sigmoid(g2_ref[...]) * _dot(orw_ref[...].astype(BF16), ww_ref[...])
    o_ref[...] = x_ref[...] + gt_ref[...] * _dot(merged.astype(BF16), wo_ref[...])


def _merge(x, gate, o_ret, o_fox, o_rwkv, p2, wr, wf, ww, wo, *, tm, rows_per_seq, gcol):
    n, d = x.shape
    row = lambda a: pl.BlockSpec((tm, a.shape[1]), lambda i: (i, 0))
    res = lambda a: pl.BlockSpec(a.shape, lambda i: (0, 0), pipeline_mode=pl.Buffered(1))
    return pl.pallas_call(
        _merge_kernel,
        out_shape=jax.ShapeDtypeStruct((n, d), F32),
        grid=(n // tm,),
        in_specs=[
            row(x), _mod_spec(gate, tm, rows_per_seq, d, 1), row(o_ret), row(o_fox), row(o_rwkv),
            pl.BlockSpec((tm, d), lambda i: (i, gcol)),
            pl.BlockSpec((tm, d), lambda i: (i, gcol + 1)),
            pl.BlockSpec((tm, d), lambda i: (i, gcol + 2)),
            res(wr), res(wf), res(ww), res(wo),
        ],
        out_specs=pl.BlockSpec((tm, d), lambda i: (i, 0)),
        compiler_params=_cparams(1, 48),
        name="merge",
    )(x, gate, o_ret, o_fox, o_rwkv, p2, p2, p2, wr, wf, ww, wo)


def _final_kernel(x_ref, g_ref, o_ref):
    x = x_ref[...]
    o_ref[...] = x * lax.rsqrt(jnp.mean(x * x, axis=-1, keepdims=True) + NORM_EPS) * g_ref[...]


def _final_norm(x, g, *, tm):
    n, d = x.shape
    return pl.pallas_call(
        _final_kernel,
        out_shape=jax.ShapeDtypeStruct((n, d), F32),
        grid=(n // tm,),
        in_specs=[pl.BlockSpec((tm, d), lambda i: (i, 0)), pl.BlockSpec((1, d), lambda i: (0, 0))],
        out_specs=pl.BlockSpec((tm, d), lambda i: (i, 0)),
        compiler_params=_cparams(1, 32),
        name="final_norm",
    )(x, g)


def _largest_tile(n, cap):
    t = min(n, cap)
    while n % t:
        t //= 2
    return t


def kernel(x_prompt, x_sample, cache_fox_k, cache_fox_v, cache_fox_logf, state_ret, state_rwkv_wkv,
           state_rwkv_shift, page_table, c_prompt, c_sample, w_ada, b_ada, norm_g, w_ffn_in, w_ffn_out,
           w_in, b_fox_f, rwkv_mu, rwkv_w0, rwkv_w2, rwkv_a0, rwkv_a2, rwkv_g2, rwkv_k_k, rwkv_k_a,
           rwkv_r_k, rwkv_ln_w, rwkv_ln_b, w_br_ret, w_br_fox, w_br_rwkv, w_o, final_norm_g):
    depth = w_in.shape[0]
    bp, tp, d = x_prompt.shape
    bs, ts, _ = x_sample.shape
    _, _, h_ret, ret_dk, ret_dv = state_ret.shape
    h_fox = b_fox_f.shape[1]
    fox_hd = cache_fox_k.shape[-1]
    _, h_rw, rw_n = rwkv_r_k.shape
    lora_w, lora_a, lora_g = rwkv_w2.shape[1], rwkv_a2.shape[1], rwkv_g2.shape[1]
    rw_c = h_rw * rw_n
    w_qk, w_v, w_fox = h_ret * ret_dk, h_ret * ret_dv, h_fox * fox_hd
    rw_in = 3 * rw_c + lora_w + lora_a + lora_g
    page = cache_fox_k.shape[2]
    n_pool = cache_fox_k.shape[1]
    past_len = page_table.shape[1] * page
    assert lora_w + lora_a == LANES and lora_g == LANES and w_qk == rw_c == w_fox and w_v == d == 2 * w_qk

    src = {}
    off = 0
    for name, size in (("rq", w_qk), ("rk", w_qk), ("rv", w_v), ("rg", w_v), ("fq", w_fox), ("fk", w_fox),
                       ("fv", w_fox), ("ff", h_fox), ("rw", rw_in), ("gates", 3 * d)):
        src[name] = (off, size)
        off += size
    ff_pad = 2 * LANES - h_fox
    order = ("rw", "ff", "rv", "rg", "gates", "rq", "rk", "fq", "fk", "fv")
    dst = {}
    off = 0
    for name in order:
        dst[name] = off
        off += src[name][1] + (ff_pad if name == "ff" else 0)
    nw = off
    cols = {"rq": dst["rq"] // w_qk, "rk": dst["rk"] // w_qk, "rv": dst["rv"] // w_v, "rg": dst["rg"] // w_v}
    fox_cols = {k: dst[k] // LANES for k in ("fq", "fk", "fv")}
    fox_cols_w = {k: dst[k] // w_fox for k in ("fq", "fk", "fv")}
    ff_col = dst["ff"] // LANES
    gcol = dst["gates"] // d
    for name, width in (("rq", w_qk), ("rk", w_qk), ("rv", w_v), ("rg", w_v), ("gates", d), ("fq", w_fox),
                        ("fk", w_fox), ("fv", w_fox), ("ff", LANES), ("rw", rw_in)):
        assert dst[name] % width == 0

    def pack_w_in(w):
        parts = []
        for name in order:
            o, s = src[name]
            parts.append(w[:, o:o + s])
            if name == "ff":
                parts.append(jnp.zeros((d, ff_pad), w.dtype))
        return jnp.concatenate(parts, axis=1).astype(BF16)

    half = ret_dk // 2
    inv = ROPE_BASE ** (-jnp.arange(half, dtype=F32) / half)

    def rope_tables(pos):
        ang = pos[:, None] * inv[None, :]
        cos, sin = jnp.cos(ang), jnp.sin(ang)
        reps = LANES // ret_dk
        return (jnp.tile(jnp.concatenate([cos, cos], axis=1), (1, reps)),
                jnp.tile(jnp.concatenate([-sin, sin], axis=1), (1, reps)))

    cos_p, sin_p = rope_tables(jnp.arange(tp, dtype=F32))
    cos_s, sin_s = rope_tables(jnp.arange(ts, dtype=F32) + past_len)

    m_all = _ada(jnp.concatenate([c_prompt, c_sample], axis=0), w_ada, b_ada)
    m_all = m_all.reshape(depth, bp + bs, 9, d)

    hid = lax.broadcasted_iota(jnp.int32, (rw_c, rw_c), 0) // rw_n
    obd = (hid == hid.T).astype(BF16)
    lf_t_all = jnp.swapaxes(cache_fox_logf, -1, -2)
    ck_t = jnp.transpose(cache_fox_k, (0, 1, 3, 4, 2))
    cv_t = jnp.transpose(cache_fox_v, (0, 1, 3, 4, 2))

    xp = x_prompt.reshape(bp * tp, d)
    xs = x_sample.reshape(bs * ts, d)
    tm_p = _largest_tile(tp, 512)
    outs_p, outs_s = [], []

    for l in range(depth):
        w1 = [w_ffn_in[l, i].astype(BF16) for i in range(2)]
        w2 = [w_ffn_out[l, i].astype(BF16) for i in range(2)]
        wp = pack_w_in(w_in[l])
        wr, wf, ww, wo = (w_br_ret[l].astype(BF16), w_br_fox[l].astype(BF16), w_br_rwkv[l].astype(BF16),
                          w_o[l].astype(BF16))
        zpad_w = jnp.zeros((lora_a, rw_c), F32)
        zpad_a = jnp.zeros((lora_w, rw_c), F32)
        row = lambda v: v.reshape(1, -1)
        rw_params = [
            row(rwkv_mu[l]), row(rwkv_w0[l]),
            jnp.concatenate([rwkv_w2[l], zpad_w], axis=0).astype(BF16), row(rwkv_a0[l]),
            jnp.concatenate([zpad_a, rwkv_a2[l]], axis=0).astype(BF16), rwkv_g2[l].astype(BF16),
            row(rwkv_k_k[l]), row(rwkv_k_a[l]), row(rwkv_r_k[l]), row(rwkv_ln_w[l]), row(rwkv_ln_b[l]), obd,
        ]
        bias128 = jnp.concatenate([b_fox_f[l], jnp.zeros((LANES - h_fox,), F32)]).reshape(1, LANES)
        ng = [norm_g[l, i].reshape(1, d) for i in range(3)]

        for group in ("prompt", "sample"):
            if group == "prompt":
                x, nb, t, m = xp, bp, tp, m_all[l, :bp]
                mods = [m[:, i].reshape(nb, 1, d) for i in range(9)]
                tm, rps = tm_p, tp
            else:
                x, nb, t, m = xs, bs, ts, m_all[l, bp:]
                mods = [jnp.repeat(m[:, i], t, axis=0) for i in range(9)]
                tm, rps = bs * ts, ts
            sh1, sc1, g1, sh2, sc2, g2, sh3, sc3, g3 = mods

            tm_big = _largest_tile(t, 1024) if group == "prompt" else tm
            x = _ffn(x, sh1, sc1, g1, ng[0], w1[0], w2[0], tm=tm_big, rows_per_seq=rps)
            tn = _largest_tile(nw, nw // 4) if (nw // 4) % LANES == 0 else LANES
            p2 = _inproj(x, sh2, sc2, ng[1], wp, tm=tm_big, tn=tn, rows_per_seq=rps)
            p3 = p2.reshape(nb, t, nw)

            x_last = x.reshape(nb, t, d)[:, -1]
            h_last = _modnorm_rows(x_last, m[:, 3], m[:, 4], ng[1])

            if group == "prompt":
                s0 = jnp.zeros((nb, h_ret, ret_dk, ret_dv), F32)
                cos, sin, lreal = cos_p, sin_p, min(LANES, t)
            else:
                s0 = state_ret[l]
                cos, sin, lreal = cos_s, sin_s, min(LANES, t)
            o_ret, ret_s = _retention(p3, cos, sin, s0, lreal=lreal, cols=cols)

            logf, cum = _fox_gate(p3, bias128, n_heads=h_fox, col=ff_col)
            if group == "prompt":
                o_fox = _fox_prompt(p3, jnp.swapaxes(cum, 1, 2), n_heads=h_fox, hd=fox_hd, cols=fox_cols_w)
            else:
                c1 = jnp.pad(jnp.swapaxes(cum, 1, 2), ((0, 0), (0, 0), (0, page - t)))
                o_fox = _fox_sample(p3, c1, ck_t, cv_t, lf_t_all, page_table, l,
                                    n_heads=h_fox, hd=fox_hd, cols=fox_cols_w)

            if group == "prompt":
                prev0 = jnp.zeros((nb, 1, rw_in), F32)
                z0p = jnp.zeros((nb, h_rw // 2, 2 * rw_n, 2 * rw_n), F32)
            else:
                prev0 = _matmul_small(state_rwkv_shift[l], wp[:, :rw_in]).reshape(nb, 1, rw_in)
                zt = jnp.swapaxes(state_rwkv_wkv[l], -1, -2).reshape(nb, h_rw // 2, 2, rw_n, rw_n)
                zz = jnp.zeros_like(zt[:, :, 0])
                z0p = jnp.concatenate([jnp.concatenate([zt[:, :, 0], zz], axis=-1),
                                       jnp.concatenate([zz, zt[:, :, 1]], axis=-1)], axis=-2)
            o_rw, zp = _rwkv(p3, prev0, z0p, rw_params, tc=min(LANES // 2, t), n_heads=h_rw, hn=rw_n, col=0)
            z_new = jnp.stack([zp[:, :, :rw_n, :rw_n], zp[:, :, rw_n:, rw_n:]], axis=2)
            rw_s = jnp.swapaxes(z_new.reshape(nb, h_rw, rw_n, rw_n), -1, -2)

            x = _merge(x, g2, o_ret.reshape(nb * t, w_v), o_fox.reshape(nb * t, w_fox), o_rw.reshape(nb * t, rw_c),
                       p2, wr, wf, ww, wo, tm=tm, rows_per_seq=rps, gcol=gcol)
            x = _ffn(x, sh3, sc3, g3, ng[2], w1[1], w2[1], tm=tm_big, rows_per_seq=rps)

            fox_k = p3[:, :, dst["fk"]:dst["fk"] + w_fox].reshape(nb, t, h_fox, fox_hd)
            fox_v = p3[:, :, dst["fv"]:dst["fv"] + w_fox].reshape(nb, t, h_fox, fox_hd)
            st = (fox_k, fox_v, logf, ret_s, rw_s, h_last)
            if group == "prompt":
                xp = x
                outs_p.append(st)
            else:
                xs = x
                outs_s.append(st)

    fng = final_norm_g.reshape(1, d)
    y_prompt = _final_norm(xp, fng, tm=_largest_tile(bp * tp, 1024)).reshape(bp, tp, d)
    y_sample = _final_norm(xs, fng, tm=bs * ts).reshape(bs, ts, d)
    stack = lambda sts, i: jnp.stack([s[i] for s in sts], axis=0)
    return (y_prompt, y_sample) + tuple(stack(outs_p, i) for i in range(6)) + tuple(stack(outs_s, i) for i in range(6))
```
